```python
import math
import jax, jax.numpy as jnp
from jax import lax
import numpy as np

D_MODEL = 1024
BATCH = 16
SEQ = 256
DEPTH = 4
DEC_BATCH = 2
DEC_SEQ = 1024
PAST_LEN = 256

GRID_W = 64
N_RG = (DEPTH + 1) // 2
N_GLA = DEPTH // 2
N_ADA = 6
EPS = 1e-6
D_RNN = D_MODEL
CONV_W = 4
CONV_LEFT = 2
RG_BLOCKS = 16
RG_BW = D_RNN // RG_BLOCKS
RG_C = 8.0
GLA_HEADS = 4
GLA_DK = D_MODEL // 2 // GLA_HEADS
GLA_DV = D_MODEL // GLA_HEADS
GLA_QK = GLA_HEADS * GLA_DK
GLA_V = GLA_HEADS * GLA_DV
GLA_RANK = 16
GLA_TAU = 16.0
GLA_CHUNK = 64
GLA_IN = 2 * GLA_QK + 2 * GLA_V + 2 * GLA_RANK
PEER_HEADS = 8
PEER_NKEYS = 128
PEER_N = PEER_NKEYS * PEER_NKEYS
PEER_DKEY = 256
PEER_HALF = PEER_DKEY // 2
PEER_TOPK = 16
PEER_BLOCK = 128

kernel_name = 'hybrid_rglru_gla_peer_diffusion_step'


def rmsnorm(x, g):
    xf = x.astype(jnp.float32)
    y = xf * lax.rsqrt(jnp.mean(xf * xf, axis=-1, keepdims=True) + EPS)
    return (y * g.astype(jnp.float32)).astype(x.dtype)


def pre_mod(x, g, shift, scale):
    return rmsnorm(x, g) * (1 + scale) + shift


def ada_mod(cond, w, b):
    m = (jax.nn.silu(cond) @ w + b).reshape(cond.shape[0], N_ADA, cond.shape[-1])
    return [m[:, i, None, :] for i in range(N_ADA)]


def grid_pos_embed(rows, dim):
    t = jnp.arange(rows * GRID_W)
    r = (t // GRID_W).astype(jnp.float32)
    col = (t % GRID_W).astype(jnp.float32)
    nf = dim // 4
    freq = 1.0 / (10000.0 ** (jnp.arange(nf, dtype=jnp.float32) / nf))
    ar = r[:, None] * freq
    ac = col[:, None] * freq
    return jnp.concatenate([jnp.sin(ar), jnp.cos(ar), jnp.sin(ac), jnp.cos(ac)], axis=-1)


def to_col_major(x, rows):
    b, n, d = x.shape
    return x.reshape(b, rows, GRID_W, d).swapaxes(1, 2).reshape(b, n, d)


def from_col_major(x, rows):
    b, n, d = x.shape
    return x.reshape(b, GRID_W, rows, d).swapaxes(1, 2).reshape(b, n, d)


def centred_conv(x, w, b):
    L = x.shape[1]
    xp = jnp.pad(x, ((0, 0), (CONV_LEFT, CONV_W - 1 - CONV_LEFT), (0, 0)))
    return sum(xp[:, j:j + L] * w[j] for j in range(CONV_W)) + b


def block_diag(x, w, b):
    xb = x.reshape(x.shape[:-1] + (RG_BLOCKS, RG_BW))
    return jnp.einsum('blni,nij->blnj', xb, w).reshape(x.shape) + b


def linear_scan(a, b, h0):
    b = b.at[:, 0].add(a[:, 0] * h0)
    def comb(l, r):
        return (l[0] * r[0], r[0] * l[1] + r[1])
    _, h = lax.associative_scan(comb, (a, b), axis=1)
    return h


def rglru_dir(xc, wa, ba, wi, bi, lam, h0):
    r = jax.nn.sigmoid(block_diag(xc, wa, ba).astype(jnp.float32))
    i = jax.nn.sigmoid(block_diag(xc, wi, bi).astype(jnp.float32))
    log_a = -RG_C * r * jax.nn.softplus(-lam.astype(jnp.float32))
    a = jnp.exp(log_a)
    b = jnp.sqrt(-jnp.expm1(2.0 * log_a)) * (i * xc)
    h = linear_scan(a, b, h0.astype(jnp.float32))
    return h, h[:, -1]


def rglru_mixer(xm, p, h0):
    w_in, conv_w, conv_b, wa, ba, wi, bi, lam, w_out = p
    proj = xm @ w_in
    gate_br = jax.nn.gelu(proj[..., :D_RNN])
    xr = centred_conv(proj[..., D_RNN:], conv_w, conv_b).astype(jnp.float32)
    hf, sf = rglru_dir(xr, wa[0], ba[0], wi[0], bi[0], lam[0], h0[:, 0])
    hb, sb = rglru_dir(xr[:, ::-1], wa[1], ba[1], wi[1], bi[1], lam[1], h0[:, 1])
    h = hf + hb[:, ::-1]
    out = (gate_br * h.astype(gate_br.dtype)) @ w_out
    return out, jnp.stack([sf, sb], axis=1)


def gla_dir(q, k, v, log_a, s0):
    B, H, L, _ = q.shape
    C = GLA_CHUNK
    n = L // C
    q, k, v, log_a = [t.reshape(B, H, n, C, t.shape[-1]) for t in (q, k, v, log_a)]
    bcum = jnp.cumsum(log_a, axis=3)
    blast = bcum[:, :, :, -1:]
    q_in = q * jnp.exp(bcum)
    k_in = k * jnp.exp(-bcum)
    k_st = k * jnp.exp(blast - bcum)
    mask = jnp.tril(jnp.ones((C, C), dtype=bool))
    att = jnp.where(mask, jnp.einsum('bhnik,bhnjk->bhnij', q_in, k_in), 0.0)
    o_intra = jnp.einsum('bhnij,bhnjv->bhniv', att, v)
    u = jnp.einsum('bhnck,bhncv->bhnkv', k_st, v)
    g = jnp.exp(blast[:, :, :, 0])
    def step(s, inp):
        g_n, u_n = inp
        return g_n[..., None] * s + u_n, s
    s_fin, s_in = lax.scan(step, s0.astype(jnp.float32), (jnp.moveaxis(g, 2, 0), jnp.moveaxis(u, 2, 0)))
    s_in = jnp.moveaxis(s_in, 0, 2)
    o = o_intra + jnp.einsum('bhnck,bhnkv->bhncv', q_in, s_in)
    return o.reshape(B, H, L, -1), s_fin


def gla_mixer(xm, p, s0):
    w_in, w_al, b_al, norm_g, w_out = p
    B, L, _ = xm.shape
    proj = xm @ w_in
    q, k, v, g, z = jnp.split(proj, [GLA_QK, 2 * GLA_QK, 2 * GLA_QK + GLA_V, 2 * GLA_QK + 2 * GLA_V], axis=-1)
    def heads(t, d):
        return t.reshape(B, L, GLA_HEADS, d).transpose(0, 2, 1, 3).astype(jnp.float32)
    qh = heads(q, GLA_DK) * (GLA_DK ** -0.5)
    kh = heads(k, GLA_DK)
    vh = heads(v, GLA_DV)
    z = z.reshape(B, L, 2, GLA_RANK).astype(jnp.float32)
    log_a = jax.nn.log_sigmoid(jnp.einsum('bldr,drk->bldk', z, w_al.astype(jnp.float32)) + b_al.astype(jnp.float32)) / GLA_TAU
    la = log_a.reshape(B, L, 2, GLA_HEADS, GLA_DK).transpose(2, 0, 3, 1, 4)
    of, sf = gla_dir(qh, kh, vh, la[0], s0[:, 0])
    flip = lambda t: t[:, :, ::-1]
    ob, sb = gla_dir(flip(qh), flip(kh), flip(vh), flip(la[1]), s0[:, 1])
    o = (of + flip(ob)).transpose(0, 2, 1, 3)
    o = o * lax.rsqrt(jnp.mean(o * o, axis=-1, keepdims=True) + EPS)
    o = o.reshape(B, L, GLA_V) * norm_g.astype(jnp.float32)
    out = (jax.nn.silu(g.astype(jnp.float32)) * o).astype(xm.dtype) @ w_out
    return out, jnp.stack([sf, sb], axis=1)


def gla_mixer_grid(xm, p, s0, rows):
    out, st = gla_mixer(to_col_major(xm, rows), p, s0)
    return from_col_major(out, rows), st


def peer(xm, wq, k1, k2, u_tab, v_tab):
    B, L, D = xm.shape
    xt = xm.reshape(-1, D)
    T = xt.shape[0]
    q = (xt @ wq).reshape(T, PEER_HEADS, 2, PEER_HALF).astype(jnp.float32)
    s1 = jnp.einsum('thd,hnd->thn', q[:, :, 0], k1.astype(jnp.float32))
    s2 = jnp.einsum('thd,hnd->thn', q[:, :, 1], k2.astype(jnp.float32))
    v1, i1 = lax.top_k(s1, PEER_TOPK)
    v2, i2 = lax.top_k(s2, PEER_TOPK)
    cand = (v1[..., :, None] + v2[..., None, :]).reshape(T, PEER_HEADS, -1)
    cid = (i1[..., :, None] * PEER_NKEYS + i2[..., None, :]).reshape(T, PEER_HEADS, -1)
    sc, pos = lax.top_k(cand, PEER_TOPK)
    eid = jnp.take_along_axis(cid, pos, axis=-1)
    gate = jax.nn.softmax(sc, axis=-1)
    nb = T // PEER_BLOCK
    def block(args):
        xb, eb, gb = args
        ub = jnp.take(u_tab, eb, axis=0)
        act = jax.nn.gelu(jnp.einsum('td,thkd->thk', xb, ub).astype(jnp.float32))
        vb = jnp.take(v_tab, eb, axis=0)
        return jnp.einsum('thk,thkd->td', (gb * act).astype(vb.dtype), vb)
    out = lax.map(block, (xt.reshape(nb, PEER_BLOCK, D),
                          eid.reshape(nb, PEER_BLOCK, PEER_HEADS, PEER_TOPK),
                          gate.reshape(nb, PEER_BLOCK, PEER_HEADS, PEER_TOPK)))
    return out.reshape(B, L, D).astype(xm.dtype)


def setup_inputs(seed: int = 0) -> dict:
    key = jax.random.key(seed)
    ks = jax.random.split(key, 32)
    nrm = lambda k, s, sc: jax.random.normal(k, s, jnp.float32) * sc
    D = D_MODEL
    u = jax.random.uniform(ks[20], (N_RG, 2, D_RNN), jnp.float32, minval=0.9, maxval=0.999)
    s = u ** (1.0 / RG_C)
    rg_lambda = jnp.log(s) - jnp.log1p(-s)
    return {
        'x_prompt': nrm(ks[0], (BATCH, SEQ, D), 1.0),
        'x_sample': nrm(ks[1], (DEC_BATCH, DEC_SEQ, D), 1.0),
        'state_rglru': nrm(ks[2], (DEC_BATCH, N_RG, 2, D_RNN), 1.0),
        'state_gla': nrm(ks[3], (DEC_BATCH, N_GLA, 2, GLA_HEADS, GLA_DK, GLA_DV), 0.5),
        'c': nrm(ks[4], (DEC_BATCH, D), 1.0),
        'c_ctx': nrm(ks[5], (D,), 1.0),
        'norm1_g': 1.0 + nrm(ks[6], (DEPTH, D), 0.02),
        'norm2_g': 1.0 + nrm(ks[7], (DEPTH, D), 0.02),
        'ada_w': nrm(ks[8], (DEPTH, D, N_ADA * D), 0.5 * D ** -0.5),
        'ada_b': nrm(ks[9], (DEPTH, N_ADA * D), 0.02),
        'peer_wq': nrm(ks[10], (DEPTH, D, PEER_HEADS * PEER_DKEY), D ** -0.5),
        'peer_k1': nrm(ks[11], (DEPTH, PEER_HEADS, PEER_NKEYS, PEER_HALF), PEER_HALF ** -0.5),
        'peer_k2': nrm(ks[12], (DEPTH, PEER_HEADS, PEER_NKEYS, PEER_HALF), PEER_HALF ** -0.5),
        'peer_u': nrm(ks[13], (DEPTH, PEER_N, D), D ** -0.5),
        'peer_v': nrm(ks[14], (DEPTH, PEER_N, D), PEER_HEADS ** -0.5),
        'rg_w_in': nrm(ks[15], (N_RG, D, 2 * D_RNN), D ** -0.5),
        'rg_conv_w': nrm(ks[16], (N_RG, CONV_W, D_RNN), CONV_W ** -0.5),
        'rg_conv_b': nrm(ks[17], (N_RG, D_RNN), 0.02),
        'rg_wa': nrm(ks[18], (N_RG, 2, RG_BLOCKS, RG_BW, RG_BW), RG_BW ** -0.5),
        'rg_ba': nrm(ks[19], (N_RG, 2, D_RNN), 0.02),
        'rg_wi': nrm(ks[21], (N_RG, 2, RG_BLOCKS, RG_BW, RG_BW), RG_BW ** -0.5),
        'rg_bi': nrm(ks[22], (N_RG, 2, D_RNN), 0.02),
        'rg_lambda': rg_lambda,
        'rg_w_out': nrm(ks[23], (N_RG, D_RNN, D), D_RNN ** -0.5),
        'gla_w_in': nrm(ks[24], (N_GLA, D, GLA_IN), D ** -0.5),
        'gla_w_alpha': nrm(ks[25], (N_GLA, 2, GLA_RANK, GLA_QK), GLA_RANK ** -0.5),
        'gla_b_alpha': nrm(ks[26], (N_GLA, 2, GLA_QK), 0.1),
        'gla_norm_g': 1.0 + nrm(ks[27], (N_GLA, GLA_V), 0.02),
        'gla_w_out': nrm(ks[28], (N_GLA, GLA_V, D), GLA_V ** -0.5),
        'final_norm_g': 1.0 + nrm(ks[29], (D,), 0.02),
    }


def reference(x_prompt, x_sample, state_rglru, state_gla, c, c_ctx, norm1_g, norm2_g, ada_w, ada_b,
              peer_wq, peer_k1, peer_k2, peer_u, peer_v, rg_w_in, rg_conv_w, rg_conv_b, rg_wa, rg_ba,
              rg_wi, rg_bi, rg_lambda, rg_w_out, gla_w_in, gla_w_alpha, gla_b_alpha, gla_norm_g,
              gla_w_out, final_norm_g):
    bp = x_prompt.shape[0]
    rows = x_sample.shape[1] // GRID_W
    xp = x_prompt
    xs = x_sample + grid_pos_embed(rows, x_sample.shape[-1]).astype(x_sample.dtype)[None]
    rg_new = []
    gla_new = []
    for l in range(DEPTH):
        mod_p = ada_mod(c_ctx[None], ada_w[l], ada_b[l])
        mod_s = ada_mod(c, ada_w[l], ada_b[l])
        j = l // 2
        hp = pre_mod(xp, norm1_g[l], mod_p[0], mod_p[1])
        hs = pre_mod(xs, norm1_g[l], mod_s[0], mod_s[1])
        if l % 2 == 0:
            p = (rg_w_in[j], rg_conv_w[j], rg_conv_b[j], rg_wa[j], rg_ba[j], rg_wi[j], rg_bi[j],
                 rg_lambda[j], rg_w_out[j])
            op, st = rglru_mixer(hp, p, jnp.zeros((bp, 2, D_RNN), jnp.float32))
            rg_new.append(st)
            os_, _ = rglru_mixer(hs, p, state_rglru[:, j])
        else:
            p = (gla_w_in[j], gla_w_alpha[j], gla_b_alpha[j], gla_norm_g[j], gla_w_out[j])
            op, st = gla_mixer(hp, p, jnp.zeros((bp, 2, GLA_HEADS, GLA_DK, GLA_DV), jnp.float32))
            gla_new.append(st)
            os_, _ = gla_mixer_grid(hs, p, state_gla[:, j], rows)
        xp = xp + mod_p[2] * op
        xs = xs + mod_s[2] * os_
        pp = (peer_wq[l], peer_k1[l], peer_k2[l], peer_u[l], peer_v[l])
        xp = xp + mod_p[5] * peer(pre_mod(xp, norm2_g[l], mod_p[3], mod_p[4]), *pp)
        xs = xs + mod_s[5] * peer(pre_mod(xs, norm2_g[l], mod_s[3], mod_s[4]), *pp)
    y_prompt = rmsnorm(xp, final_norm_g)
    y_sample = rmsnorm(xs, final_norm_g)
    new_state_rglru = jnp.stack(rg_new, axis=1)
    new_state_gla = jnp.stack(gla_new, axis=1)
    return (y_prompt, y_sample, new_state_rglru, new_state_gla)
```

```python
import functools
import math

import jax
import jax.numpy as jnp
from jax import lax
from jax.experimental import pallas as pl
from jax.experimental.pallas import tpu as pltpu

f32 = jnp.float32
bf16 = jnp.bfloat16

D = 1024
N_PROMPT_SEQ, PROMPT_LEN = 16, 256
N_SAMPLE_SEQ, SAMPLE_LEN = 2, 1024
NP = N_PROMPT_SEQ * PROMPT_LEN
NS = N_SAMPLE_SEQ * SAMPLE_LEN
T = NP + NS
DEPTH = 4
GRID_W = 64
N_ADA = 6
N_GROUPS = 1 + N_SAMPLE_SEQ
EPS = 1e-6
CONV_W, CONV_LEFT = 4, 2
RG_BLOCKS, RG_BW = 16, 64
RG_C = 8.0
GLA_HEADS, GLA_DK, GLA_DV = 4, 128, 256
GLA_QK, GLA_V = 512, 1024
GLA_RANK = 16
GLA_TAU = 16.0
GLA_CHUNK = 64
GLA_IN = 2 * GLA_QK + 2 * GLA_V + 2 * GLA_RANK
GLA_IN_PAD = 2 * GLA_QK + 2 * GLA_V + 128
PEER_HEADS, PEER_NKEYS, PEER_HALF, PEER_TOPK = 8, 128, 128, 16
PEER_N = PEER_NKEYS * PEER_NKEYS

LANE = 128
SUBLANE = 8
VMEM_LIMIT = 52 * 1024 * 1024

TM = 256
RG_SEQ = 1024
RG_TC = 256
PEER_TM = 512
PEER_TE = 1024
CAND = [(a, b) for a in range(PEER_TOPK) for b in range(PEER_TOPK) if (a + 1) * (b + 1) <= PEER_TOPK + 1]
CAND_ROWS = -(-len(CAND) // SUBLANE) * SUBLANE


def _cparams(sem):
    return pltpu.CompilerParams(dimension_semantics=sem, vmem_limit_bytes=VMEM_LIMIT)


def _tile_group(i):
    return jnp.where(i < NP // TM, 0, 1 + (i - NP // TM) // (SAMPLE_LEN // TM))


def _gelu(x):
    return 0.5 * x * (1.0 + jnp.tanh(0.7978845608028654 * (x + 0.044715 * (x * x * x))))


def _sigmoid(x):
    return 1.0 / (1.0 + jnp.exp(-x))


def _softplus(x):
    return jnp.maximum(x, 0.0) + jnp.log1p(jnp.exp(-jnp.abs(x)))


def _normmod(x, g, shift, scale):
    ms = jnp.mean(x * x, axis=-1, keepdims=True)
    y = (x * lax.rsqrt(ms + EPS)) * g
    return y * (1.0 + scale) + shift


def _ada_kernel(ct_ref, w_ref, b_ref, o_ref):
    ct = ct_ref[...]
    s = ct * _sigmoid(ct)
    w = w_ref[...]
    rows = [jnp.sum(w * s[:, r:r + 1], axis=0, keepdims=True) for r in range(N_GROUPS)]
    o_ref[...] = jnp.concatenate(rows, axis=0) + b_ref[...]


def _ada_all(cond_t, ada_w, ada_b):
    tn = 1536
    nd = N_ADA * D
    return pl.pallas_call(
        _ada_kernel,
        grid=(DEPTH, nd // tn),
        in_specs=[pl.BlockSpec((D, SUBLANE), lambda l, j: (0, 0)),
                  pl.BlockSpec((None, D, tn), lambda l, j: (l, 0, j)),
                  pl.BlockSpec((None, 1, tn), lambda l, j: (l, 0, j))],
        out_specs=pl.BlockSpec((None, N_GROUPS, tn), lambda l, j: (l, 0, j)),
        out_shape=jax.ShapeDtypeStruct((DEPTH, N_GROUPS, nd), f32),
        compiler_params=_cparams(("parallel", "parallel")),
        name="ada_mod",
    )(cond_t, ada_w, ada_b.reshape(DEPTH, 1, nd))


def _nm_matmul_kernel(x_ref, g_ref, sh_ref, sc_ref, w_ref, o_ref):
    xm = _normmod(x_ref[...], g_ref[...], sh_ref[...], sc_ref[...])
    o_ref[...] = jnp.dot(xm.astype(bf16), w_ref[...], preferred_element_type=f32)


def _nm_matmul(x, g, shift, scale, w, name):
    n = w.shape[1]
    return pl.pallas_call(
        _nm_matmul_kernel,
        grid=(T // TM,),
        in_specs=[pl.BlockSpec((TM, D), lambda i: (i, 0)),
                  pl.BlockSpec((1, D), lambda i: (0, 0)),
                  pl.BlockSpec((None, 1, D), lambda i: (_tile_group(i), 0, 0)),
                  pl.BlockSpec((None, 1, D), lambda i: (_tile_group(i), 0, 0)),
                  pl.BlockSpec((D, n), lambda i: (0, 0))],
        out_specs=pl.BlockSpec((TM, n), lambda i: (i, 0)),
        out_shape=jax.ShapeDtypeStruct((T, n), f32),
        compiler_params=_cparams(("parallel",)),
        name=name,
    )(x, g, shift, scale, w)


def _res_matmul_kernel(y_ref, w_ref, x_ref, gm_ref, o_ref):
    d = jnp.dot(y_ref[...], w_ref[...], preferred_element_type=f32)
    o_ref[...] = x_ref[...] + gm_ref[...] * d


def _res_matmul(y, w, x, gmod, name):
    k = y.shape[1]
    return pl.pallas_call(
        _res_matmul_kernel,
        grid=(T // TM,),
        in_specs=[pl.BlockSpec((TM, k), lambda i: (i, 0)),
                  pl.BlockSpec((k, D), lambda i: (0, 0)),
                  pl.BlockSpec((TM, D), lambda i: (i, 0)),
                  pl.BlockSpec((None, 1, D), lambda i: (_tile_group(i), 0, 0))],
        out_specs=pl.BlockSpec((TM, D), lambda i: (i, 0)),
        out_shape=jax.ShapeDtypeStruct((T, D), f32),
        compiler_params=_cparams(("parallel",)),
        name=name,
    )(y, w, x, gmod)


def _rg_kernel(gp_ref, xp_ref, cw_ref, cb_ref, w4_ref, b4_ref, lam_ref, h0_ref, y_ref, fin_ref,
               pad, a_f, b_f, a_b, b_b, h_f, h_b):
    L, tc = RG_SEQ, RG_TC
    is_prompt = pl.program_id(0) < NP // RG_SEQ
    far = jnp.where(is_prompt, 0, L)
    pad[0:SUBLANE, :] = jnp.zeros((SUBLANE, tc), f32)
    pad[SUBLANE + L:2 * SUBLANE + L, :] = jnp.zeros((SUBLANE, tc), f32)
    pad[SUBLANE:SUBLANE + L, :] = xp_ref[...]
    pos = lax.broadcasted_iota(jnp.int32, (L, tc), 0) & (PROMPT_LEN - 1)
    xr = jnp.zeros((L, tc), f32) + cb_ref[...]
    for j in range(CONV_W):
        off = j - CONV_LEFT
        xs = pad[pl.ds(SUBLANE + off, L), :]
        if off < 0:
            xs = jnp.where(pos + far >= -off, xs, 0.0)
        elif off > 0:
            xs = jnp.where(pos - far < PROMPT_LEN - off, xs, 0.0)
        xr = xr + xs * cw_ref[j:j + 1, :]
    pre = jnp.dot(xr.astype(bf16), w4_ref[...], preferred_element_type=f32) + b4_ref[...]
    sp = _softplus(-lam_ref[...])
    for d, (a_ref, b_ref, reset_pos) in enumerate(((a_f, b_f, 0), (a_b, b_b, PROMPT_LEN - 1))):
        r = _sigmoid(pre[:, (2 * d) * tc:(2 * d + 1) * tc])
        i = _sigmoid(pre[:, (2 * d + 1) * tc:(2 * d + 2) * tc])
        a = jnp.exp(-RG_C * r * sp[d:d + 1, :])
        b = jnp.sqrt(1.0 - a * a) * (i * xr)
        a_ref[...] = jnp.where(pos + far == reset_pos, 0.0, a)
        b_ref[...] = b

    def body(t, carry):
        hf, hb = carry
        base = pl.multiple_of(t * SUBLANE, SUBLANE)
        rbase = pl.multiple_of(L - SUBLANE - t * SUBLANE, SUBLANE)
        for k in range(SUBLANE):
            rf = base + k
            hf = a_f[pl.ds(rf, 1), :] * hf + b_f[pl.ds(rf, 1), :]
            h_f[pl.ds(rf, 1), :] = hf
            rb = rbase + (SUBLANE - 1 - k)
            hb = a_b[pl.ds(rb, 1), :] * hb + b_b[pl.ds(rb, 1), :]
            h_b[pl.ds(rb, 1), :] = hb
        return hf, hb

    h0 = h0_ref[...]
    lax.fori_loop(0, L // SUBLANE, body, (h0[0:1, :], h0[1:2, :]))
    y_ref[...] = (_gelu(gp_ref[...]) * (h_f[...] + h_b[...])).astype(bf16)
    for q in range(L // PROMPT_LEN):
        fin_ref[q:q + 1, :] = h_f[(q + 1) * PROMPT_LEN - 1:(q + 1) * PROMPT_LEN, :]
        fin_ref[4 + q:5 + q, :] = h_b[q * PROMPT_LEN:q * PROMPT_LEN + 1, :]


def _rg_scan(proj, conv_w, conv_b, w4, b4, lam, h0):
    ncb = D // RG_TC
    L = RG_SEQ
    return pl.pallas_call(
        _rg_kernel,
        grid=(T // L, ncb),
        in_specs=[pl.BlockSpec((L, RG_TC), lambda s, c: (s, c)),
                  pl.BlockSpec((L, RG_TC), lambda s, c: (s, ncb + c)),
                  pl.BlockSpec((CONV_W, RG_TC), lambda s, c: (0, c)),
                  pl.BlockSpec((1, RG_TC), lambda s, c: (0, c)),
                  pl.BlockSpec((None, RG_TC, 4 * RG_TC), lambda s, c: (c, 0, 0)),
                  pl.BlockSpec((None, 1, 4 * RG_TC), lambda s, c: (c, 0, 0)),
                  pl.BlockSpec((2, RG_TC), lambda s, c: (0, c)),
                  pl.BlockSpec((None, 2, RG_TC), lambda s, c: (s, 0, c))],
        out_specs=[pl.BlockSpec((L, RG_TC), lambda s, c: (s, c)),
                   pl.BlockSpec((None, SUBLANE, RG_TC), lambda s, c: (s, 0, c))],
        out_shape=[jax.ShapeDtypeStruct((T, D), bf16),
                   jax.ShapeDtypeStruct((T // L, SUBLANE, D), f32)],
        scratch_shapes=[pltpu.VMEM((L + 2 * SUBLANE, RG_TC), f32)] + [pltpu.VMEM((L, RG_TC), f32)] * 6,
        compiler_params=_cparams(("parallel", "parallel")),
        name="rg_scan",
    )(proj, proj, conv_w, conv_b, w4, b4, lam, h0)


def _rg_gate_weights(wa, ba, wi, bi):
    ncb = D // RG_TC
    per = RG_TC // RG_BW
    eye = jnp.eye(per, dtype=f32)

    def bd(w):
        w = w.reshape(ncb, per, RG_BW, RG_BW)
        return jnp.einsum('cpij,pq->cpiqj', w, eye).reshape(ncb, RG_TC, RG_TC)

    w4 = jnp.concatenate([bd(wa[0]), bd(wi[0]), bd(wa[1]), bd(wi[1])], axis=-1).astype(bf16)
    b4 = jnp.concatenate([v.reshape(ncb, 1, RG_TC) for v in (ba[0], bi[0], ba[1], bi[1])], axis=-1)
    return w4, b4


def _gla_kernel(*refs, L, has_s0):
    if has_s0:
        (q_ref, k_ref, v_ref, g_ref, z_ref, wal_ref, bal_ref, ng_ref, s0_ref,
         o_ref, sf_ref, la_f, la_b, oacc, s_scr) = refs
    else:
        (q_ref, k_ref, v_ref, g_ref, z_ref, wal_ref, bal_ref, ng_ref,
         o_ref, sf_ref, la_f, la_b, oacc, s_scr) = refs
    C = GLA_CHUNK
    n = L // C
    z = z_ref[...]
    for d, la_ref in enumerate((la_f, la_b)):
        pre = jnp.dot(z, wal_ref[d], precision=lax.Precision.HIGHEST, preferred_element_type=f32) + bal_ref[d]
        la_ref[...] = -_softplus(-pre) * (1.0 / GLA_TAU)
    ri = lax.broadcasted_iota(jnp.int32, (C, C), 0)
    ci = lax.broadcasted_iota(jnp.int32, (C, C), 1)

    def run(d, la_ref, first):
        causal = (ri >= ci) if d == 0 else (ci >= ri)
        ones_tri = causal.astype(f32)
        s_scr[...] = s0_ref[d] if has_s0 else jnp.zeros((GLA_DK, GLA_DV), f32)

        def chunk(t, _):
            c = t if d == 0 else n - 1 - t
            r0 = pl.multiple_of(c * C, C)
            la = la_ref[pl.ds(r0, C), :]
            bcum = jnp.dot(ones_tri, la, precision=lax.Precision.HIGHEST, preferred_element_type=f32)
            edge = (C - SUBLANE) if d == 0 else 0
            tot8 = bcum[edge:edge + SUBLANE, :]
            tot_row = tot8[SUBLANE - 1:SUBLANE, :] if d == 0 else tot8[0:1, :]
            g_col = jnp.exp(tot8).T[:, (SUBLANE - 1 if d == 0 else 0):(SUBLANE if d == 0 else 1)]
            q = q_ref[pl.ds(r0, C), :] * (GLA_DK ** -0.5)
            k = k_ref[pl.ds(r0, C), :]
            v = v_ref[pl.ds(r0, C), :].astype(bf16)
            q_in = (q * jnp.exp(bcum)).astype(bf16)
            k_in = (k * jnp.exp(-bcum)).astype(bf16)
            k_st = (k * jnp.exp(tot_row - bcum)).astype(bf16)
            att = lax.dot_general(q_in, k_in, (((1,), (1,)), ((), ())), preferred_element_type=f32)
            att = jnp.where(causal, att, 0.0).astype(bf16)
            s = s_scr[...]
            o = (jnp.dot(att, v, preferred_element_type=f32)
                 + jnp.dot(q_in, s.astype(bf16), preferred_element_type=f32))
            u = lax.dot_general(k_st, v, (((0,), (0,)), ((), ())), preferred_element_type=f32)
            s_scr[...] = g_col * s + u
            if first:
                oacc[pl.ds(r0, C), :] = o
            else:
                oacc[pl.ds(r0, C), :] = oacc[pl.ds(r0, C), :] + o
            return 0

        lax.fori_loop(0, n, chunk, 0)
        sf_ref[d] = s_scr[...]

    run(0, la_f, True)
    run(1, la_b, False)
    o = oacc[...]
    o = o * lax.rsqrt(jnp.mean(o * o, axis=-1, keepdims=True) + EPS) * ng_ref[...]
    g = g_ref[...]
    o_ref[...] = (g * _sigmoid(g) * o).astype(bf16)


def _gla_attn(proj, wal, bal, ng, s0, n_seq, L):
    has_s0 = s0 is not None
    H = GLA_HEADS
    in_specs = [pl.BlockSpec((L, GLA_DK), lambda b, h: (b, h)),
                pl.BlockSpec((L, GLA_DK), lambda b, h: (b, H + h)),
                pl.BlockSpec((L, GLA_DV), lambda b, h: (b, GLA_V // GLA_DV + h)),
                pl.BlockSpec((L, GLA_DV), lambda b, h: (b, 2 * GLA_V // GLA_DV + h)),
                pl.BlockSpec((L, LANE), lambda b, h: (b, (2 * GLA_QK + 2 * GLA_V) // LANE)),
                pl.BlockSpec((2, LANE, GLA_DK), lambda b, h: (0, 0, h)),
                pl.BlockSpec((2, 1, GLA_DK), lambda b, h: (0, 0, h)),
                pl.BlockSpec((1, GLA_DV), lambda b, h: (0, h))]
    args = [proj, proj, proj, proj, proj, wal, bal, ng]
    if has_s0:
        in_specs.append(pl.BlockSpec((None, 2, None, GLA_DK, GLA_DV), lambda b, h: (b, 0, h, 0, 0)))
        args.append(s0)
    return pl.pallas_call(
        functools.partial(_gla_kernel, L=L, has_s0=has_s0),
        grid=(n_seq, H),
        in_specs=in_specs,
        out_specs=[pl.BlockSpec((L, GLA_DV), lambda b, h: (b, h)),
                   pl.BlockSpec((None, 2, None, GLA_DK, GLA_DV), lambda b, h: (b, 0, h, 0, 0))],
        out_shape=[jax.ShapeDtypeStruct((n_seq * L, GLA_V), bf16),
                   jax.ShapeDtypeStruct((n_seq, 2, H, GLA_DK, GLA_DV), f32)],
        scratch_shapes=[pltpu.VMEM((L, GLA_DK), f32), pltpu.VMEM((L, GLA_DK), f32),
                        pltpu.VMEM((L, GLA_DV), f32), pltpu.VMEM((GLA_DK, GLA_DV), f32)],
        compiler_params=_cparams(("parallel", "parallel")),
        name="gla_attn_s0" if has_s0 else "gla_attn",
    )(*args)


def _top_vals(s, n):
    out = []
    cur = s
    for _ in range(n):
        m = jnp.max(cur, axis=0, keepdims=True)
        out.append(m)
        cur = jnp.where(cur == m, -jnp.inf, cur)
    return out


def _peer_route_kernel(x_ref, g_ref, sh_ref, sc_ref, wqt_ref, k1_ref, k2_ref,
                       xmt_ref, thr1_ref, e1_ref, s2m_ref, e2_ref, qt, cand):
    xm = _normmod(x_ref[...], g_ref[...], sh_ref[...], sc_ref[...])
    xmt = xm.T.astype(bf16)
    xmt_ref[...] = xmt
    qt[...] = jnp.dot(wqt_ref[...], xmt, preferred_element_type=f32)
    cand[...] = jnp.full((CAND_ROWS, TM), -jnp.inf, f32)

    def head(h, _):
        r0 = pl.multiple_of(h * 2 * PEER_HALF, 2 * PEER_HALF)
        q1 = qt[pl.ds(r0, PEER_HALF), :].astype(bf16)
        q2 = qt[pl.ds(r0 + PEER_HALF, PEER_HALF), :].astype(bf16)
        s1 = jnp.dot(k1_ref[h], q1, preferred_element_type=f32)
        s2 = jnp.dot(k2_ref[h], q2, preferred_element_type=f32)
        v1 = _top_vals(s1, PEER_TOPK)
        v2 = _top_vals(s2, PEER_TOPK)
        for idx, (a, b) in enumerate(CAND):
            cand[idx:idx + 1, :] = v1[a] + v2[b]
        m = _top_vals(cand[...], PEER_TOPK + 1)
        z = jnp.zeros((1, TM), f32)
        for kk in range(PEER_TOPK):
            z = z + jnp.exp(m[kk] - m[0])
        tau = 0.5 * (m[PEER_TOPK - 1] + m[PEER_TOPK])
        thr1_ref[h] = jnp.where(s1 >= v1[PEER_TOPK - 1], tau - s1, jnp.inf)
        e1_ref[h] = jnp.exp(s1 - v1[0])
        s2m_ref[h] = jnp.where(s2 >= v2[PEER_TOPK - 1], s2, -jnp.inf)
        e2_ref[h] = jnp.exp(s2 - v2[0]) / z
        return 0

    lax.fori_loop(0, PEER_HEADS, head, 0)


def _peer_route(x, g, shift, scale, wqt, k1, k2):
    side = jax.ShapeDtypeStruct((PEER_HEADS, PEER_NKEYS, T), f32)
    side_spec = pl.BlockSpec((PEER_HEADS, PEER_NKEYS, TM), lambda i: (0, 0, i))
    return pl.pallas_call(
        _peer_route_kernel,
        grid=(T // TM,),
        in_specs=[pl.BlockSpec((TM, D), lambda i: (i, 0)),
                  pl.BlockSpec((1, D), lambda i: (0, 0)),
                  pl.BlockSpec((None, 1, D), lambda i: (_tile_group(i), 0, 0)),
                  pl.BlockSpec((None, 1, D), lambda i: (_tile_group(i), 0, 0)),
                  pl.BlockSpec((2 * PEER_HEADS * PEER_HALF, D), lambda i: (0, 0)),
                  pl.BlockSpec((PEER_HEADS, PEER_NKEYS, PEER_HALF), lambda i: (0, 0, 0)),
                  pl.BlockSpec((PEER_HEADS, PEER_NKEYS, PEER_HALF), lambda i: (0, 0, 0))],
        out_specs=[pl.BlockSpec((D, TM), lambda i: (0, i)), side_spec, side_spec, side_spec, side_spec],
        out_shape=[jax.ShapeDtypeStruct((D, T), bf16), side, side, side, side],
        scratch_shapes=[pltpu.VMEM((2 * PEER_HEADS * PEER_HALF, TM), f32), pltpu.VMEM((CAND_ROWS, TM), f32)],
        compiler_params=_cparams(("parallel",)),
        name="peer_route",
    )(x, g, shift, scale, wqt, k1, k2)


PEER_N2T = 32
PEER_LC = LANE


def _peer_dense_kernel(xmt_ref, u_ref, vt_ref, thr1_ref, e1_ref, s2m_ref, e2_ref, x_ref, gm_ref, o_ref,
                       st, at, acc):
    j = pl.program_id(1)

    @pl.when(j == 0)
    def _():
        acc[...] = jnp.zeros_like(acc)

    st[...] = jnp.dot(u_ref[...], xmt_ref[...], preferred_element_type=f32)
    n1_blocks = PEER_TE // PEER_NKEYS
    n_lc = PEER_TM // PEER_LC

    def tile(idx, _):
        n2_0 = pl.multiple_of(idx * PEER_N2T, PEER_N2T)
        for lc in range(n_lc):
            lanes = slice(lc * PEER_LC, (lc + 1) * PEER_LC)
            gates = [jnp.zeros((PEER_N2T, PEER_LC), f32) for _ in range(n1_blocks)]
            for h in range(PEER_HEADS):
                s2 = s2m_ref[h, pl.ds(n2_0, PEER_N2T), lanes]
                e2 = e2_ref[h, pl.ds(n2_0, PEER_N2T), lanes]
                for a in range(n1_blocks):
                    thr = thr1_ref[h, a:a + 1, lanes]
                    e1 = e1_ref[h, a:a + 1, lanes]
                    gates[a] = gates[a] + jnp.where(s2 >= thr, e2 * e1, 0.0)
            for a in range(n1_blocks):
                r0 = a * PEER_NKEYS + n2_0
                s = st[pl.ds(r0, PEER_N2T), lanes]
                at[pl.ds(r0, PEER_N2T), lanes] = (gates[a] * _gelu(s)).astype(bf16)
        return 0

    lax.fori_loop(0, PEER_NKEYS // PEER_N2T, tile, 0)
    acc[...] += jnp.dot(vt_ref[...], at[...], preferred_element_type=f32)

    @pl.when(j == pl.num_programs(1) - 1)
    def _():
        o_ref[...] = x_ref[...] + gm_ref[...] * acc[...].T


def _peer_tile_group(i):
    return jnp.where(i < NP // PEER_TM, 0, 1 + (i - NP // PEER_TM) // (SAMPLE_LEN // PEER_TM))


def _peer_dense(xmt, u16, vt16, thr1, e1, s2m, e2, x, gmod):
    n1b = PEER_TE // PEER_NKEYS
    row_spec = pl.BlockSpec((PEER_HEADS, n1b, PEER_TM), lambda i, j: (0, j, i))
    full_spec = pl.BlockSpec((PEER_HEADS, PEER_NKEYS, PEER_TM), lambda i, j: (0, 0, i))
    return pl.pallas_call(
        _peer_dense_kernel,
        grid=(T // PEER_TM, PEER_N // PEER_TE),
        in_specs=[pl.BlockSpec((D, PEER_TM), lambda i, j: (0, i)),
                  pl.BlockSpec((PEER_TE, D), lambda i, j: (j, 0)),
                  pl.BlockSpec((D, PEER_TE), lambda i, j: (0, j)),
                  row_spec, row_spec, full_spec, full_spec,
                  pl.BlockSpec((PEER_TM, D), lambda i, j: (i, 0)),
                  pl.BlockSpec((None, 1, D), lambda i, j: (_peer_tile_group(i), 0, 0))],
        out_specs=pl.BlockSpec((PEER_TM, D), lambda i, j: (i, 0)),
        out_shape=jax.ShapeDtypeStruct((T, D), f32),
        scratch_shapes=[pltpu.VMEM((PEER_TE, PEER_TM), f32), pltpu.VMEM((PEER_TE, PEER_TM), bf16),
                        pltpu.VMEM((D, PEER_TM), f32)],
        compiler_params=_cparams(("parallel", "arbitrary")),
        name="peer_dense",
    )(xmt, u16, vt16, thr1, e1, s2m, e2, x, gmod)


def _final_norm_kernel(x_ref, g_ref, o_ref):
    x = x_ref[...]
    o_ref[...] = (x * lax.rsqrt(jnp.mean(x * x, axis=-1, keepdims=True) + EPS)) * g_ref[...]


def _final_norm(x, g):
    return pl.pallas_call(
        _final_norm_kernel,
        grid=(T // TM,),
        in_specs=[pl.BlockSpec((TM, D), lambda i: (i, 0)), pl.BlockSpec((1, D), lambda i: (0, 0))],
        out_specs=pl.BlockSpec((TM, D), lambda i: (i, 0)),
        out_shape=jax.ShapeDtypeStruct((T, D), f32),
        compiler_params=_cparams(("parallel",)),
        name="final_norm",
    )(x, g)


def _grid_pos_embed(rows, dim):
    t = jnp.arange(rows * GRID_W)
    r = (t // GRID_W).astype(f32)
    col = (t % GRID_W).astype(f32)
    nf = dim // 4
    freq = 1.0 / (10000.0 ** (jnp.arange(nf, dtype=f32) / nf))
    ar = r[:, None] * freq
    ac = col[:, None] * freq
    return jnp.concatenate([jnp.sin(ar), jnp.cos(ar), jnp.sin(ac), jnp.cos(ac)], axis=-1)


def _to_col_major(x, rows):
    b, n, d = x.shape
    return x.reshape(b, rows, GRID_W, d).swapaxes(1, 2).reshape(b, n, d)


def _from_col_major(x, rows):
    b, n, d = x.shape
    return x.reshape(b, GRID_W, rows, d).swapaxes(1, 2).reshape(b, n, d)


def kernel(x_prompt, x_sample, state_rglru, state_gla, c, c_ctx, norm1_g, norm2_g, ada_w, ada_b, peer_wq, peer_k1, peer_k2, peer_u, peer_v, rg_w_in, rg_conv_w, rg_conv_b, rg_wa, rg_ba, rg_wi, rg_bi, rg_lambda, rg_w_out, gla_w_in, gla_w_alpha, gla_b_alpha, gla_norm_g, gla_w_out, final_norm_g):
    rows = SAMPLE_LEN // GRID_W
    xs = x_sample + _grid_pos_embed(rows, D)[None]
    x = jnp.concatenate([x_prompt.reshape(NP, D), xs.reshape(NS, D)], axis=0)

    cond = jnp.concatenate([c_ctx[None], c, jnp.zeros((SUBLANE - N_GROUPS, D), f32)], axis=0)
    mods = _ada_all(cond.T, ada_w, ada_b).reshape(DEPTH, N_GROUPS, N_ADA, 1, D)

    rg_new, gla_new = [], []
    for l in range(DEPTH):
        j = l // 2
        mod = [mods[l, :, i] for i in range(N_ADA)]
        if l % 2 == 0:
            proj = _nm_matmul(x, norm1_g[l][None], mod[0], mod[1], rg_w_in[j].astype(bf16), "rg_in")
            w4, b4 = _rg_gate_weights(rg_wa[j], rg_ba[j], rg_wi[j], rg_bi[j])
            h0 = jnp.concatenate([jnp.zeros((NP // RG_SEQ, 2, D), f32), state_rglru[:, j]], axis=0)
            y, fin = _rg_scan(proj, rg_conv_w[j], rg_conv_b[j][None], w4, b4, rg_lambda[j], h0)
            st = fin[:NP // RG_SEQ].reshape(NP // RG_SEQ, 2, RG_SEQ // PROMPT_LEN, D)
            rg_new.append(st.transpose(0, 2, 1, 3).reshape(N_PROMPT_SEQ, 2, D))
            x = _res_matmul(y, rg_w_out[j].astype(bf16), x, mod[2], "rg_out")
        else:
            w_in = jnp.concatenate([gla_w_in[j], jnp.zeros((D, GLA_IN_PAD - GLA_IN), f32)], axis=1).astype(bf16)
            proj = _nm_matmul(x, norm1_g[l][None], mod[0], mod[1], w_in, "gla_in")
            wal = jnp.zeros((2, LANE, GLA_QK), f32)
            wal = wal.at[0, 0:GLA_RANK].set(gla_w_alpha[j, 0]).at[1, GLA_RANK:2 * GLA_RANK].set(gla_w_alpha[j, 1])
            bal = gla_b_alpha[j][:, None, :]
            ng = gla_norm_g[j][None]
            o_p, st = _gla_attn(proj, wal, bal, ng, None, N_PROMPT_SEQ, PROMPT_LEN)
            gla_new.append(st)
            proj_s = _to_col_major(proj[NP:].reshape(N_SAMPLE_SEQ, SAMPLE_LEN, GLA_IN_PAD), rows)
            o_s, _ = _gla_attn(proj_s.reshape(NS, GLA_IN_PAD), wal, bal, ng, state_gla[:, j], N_SAMPLE_SEQ, SAMPLE_LEN)
            o_s = _from_col_major(o_s.reshape(N_SAMPLE_SEQ, SAMPLE_LEN, GLA_V), rows).reshape(NS, GLA_V)
            y = jnp.concatenate([o_p[:NP], o_s], axis=0)
            x = _res_matmul(y, gla_w_out[j].astype(bf16), x, mod[2], "gla_out")
        xmt, thr1, e1, s2m, e2 = _peer_route(x, norm2_g[l][None], mod[3], mod[4], peer_wq[l].T.astype(bf16),
                                             peer_k1[l].astype(bf16), peer_k2[l].astype(bf16))
        x = _peer_dense(xmt, peer_u[l].astype(bf16), peer_v[l].T.astype(bf16), thr1, e1, s2m, e2, x, mod[5])

    y = _final_norm(x, final_norm_g[None])
    y_prompt = y[:NP].reshape(N_PROMPT_SEQ, PROMPT_LEN, D)
    y_sample = y[NP:].reshape(N_SAMPLE_SEQ, SAMPLE_LEN, D)
    return (y_prompt, y_sample, jnp.stack(rg_new, axis=1), jnp.stack(gla_new, axis=1))
```

```python
import functools
import math

import jax
import jax.numpy as jnp
from jax import lax
from jax.experimental import pallas as pl
from jax.experimental.pallas import tpu as pltpu

f32 = jnp.float32
bf16 = jnp.bfloat16

D = 1024
N_PROMPT_SEQ, PROMPT_LEN = 16, 256
N_SAMPLE_SEQ, SAMPLE_LEN = 2, 1024
NP = N_PROMPT_SEQ * PROMPT_LEN
NS = N_SAMPLE_SEQ * SAMPLE_LEN
T = NP + NS
DEPTH = 4
GRID_W = 64
N_ADA = 6
N_GROUPS = 1 + N_SAMPLE_SEQ
EPS = 1e-6
CONV_W, CONV_LEFT = 4, 2
RG_BLOCKS, RG_BW = 16, 64
RG_C = 8.0
GLA_HEADS, GLA_DK, GLA_DV = 4, 128, 256
GLA_QK, GLA_V = 512, 1024
GLA_RANK = 16
GLA_TAU = 16.0
GLA_CHUNK = 64
GLA_IN = 2 * GLA_QK + 2 * GLA_V + 2 * GLA_RANK
GLA_IN_PAD = 2 * GLA_QK + 2 * GLA_V + 128
PEER_HEADS, PEER_NKEYS, PEER_HALF, PEER_TOPK = 8, 128, 128, 16
PEER_N = PEER_NKEYS * PEER_NKEYS

LANE = 128
SUBLANE = 8
VMEM_LIMIT = 52 * 1024 * 1024

TM = 256
RG_SEQ = 1024
RG_TC = 512
PEER_TM = 512
PEER_TE = 1024
CAND = [(a, b) for a in range(PEER_TOPK) for b in range(PEER_TOPK) if (a + 1) * (b + 1) <= PEER_TOPK]
CAND_ROWS = -(-len(CAND) // SUBLANE) * SUBLANE


def _cparams(sem, flags=None):
    return pltpu.CompilerParams(dimension_semantics=sem, vmem_limit_bytes=VMEM_LIMIT, flags=flags)


def _tile_group(i):
    return jnp.where(i < NP // TM, 0, 1 + (i - NP // TM) // (SAMPLE_LEN // TM))


def _gelu(x):
    return 0.5 * x * (1.0 + jnp.tanh(0.7978845608028654 * (x + 0.044715 * (x * x * x))))


def _sigmoid(x):
    return 1.0 / (1.0 + jnp.exp(-x))


def _softplus(x):
    return jnp.maximum(x, 0.0) + jnp.log1p(jnp.exp(-jnp.abs(x)))


def _normmod(x, g, shift, scale):
    ms = jnp.mean(x * x, axis=-1, keepdims=True)
    y = (x * lax.rsqrt(ms + EPS)) * g
    return y * (1.0 + scale) + shift


def _ada_kernel(ct_ref, w_ref, b_ref, o_ref):
    ct = ct_ref[...]
    s = ct * _sigmoid(ct)
    w = w_ref[...]
    rows = [jnp.sum(w * s[:, r:r + 1], axis=0, keepdims=True) for r in range(N_GROUPS)]
    o_ref[...] = jnp.concatenate(rows, axis=0) + b_ref[...]


def _ada_all(cond_t, ada_w, ada_b):
    tn = 1536
    nd = N_ADA * D
    return pl.pallas_call(
        _ada_kernel,
        grid=(DEPTH, nd // tn),
        in_specs=[pl.BlockSpec((D, SUBLANE), lambda l, j: (0, 0)),
                  pl.BlockSpec((None, D, tn), lambda l, j: (l, 0, j)),
                  pl.BlockSpec((None, 1, tn), lambda l, j: (l, 0, j))],
        out_specs=pl.BlockSpec((None, N_GROUPS, tn), lambda l, j: (l, 0, j)),
        out_shape=jax.ShapeDtypeStruct((DEPTH, N_GROUPS, nd), f32),
        compiler_params=_cparams(("parallel", "parallel")),
        name="ada_mod",
    )(cond_t, ada_w, ada_b.reshape(DEPTH, 1, nd))


def _nm_matmul_kernel(x_ref, g_ref, sh_ref, sc_ref, w_ref, o_ref):
    xm = _normmod(x_ref[...], g_ref[...], sh_ref[...], sc_ref[...])
    o_ref[...] = jnp.dot(xm.astype(bf16), w_ref[...], preferred_element_type=f32)


def _nm_matmul(x, g, shift, scale, w, name):
    n = w.shape[1]
    return pl.pallas_call(
        _nm_matmul_kernel,
        grid=(T // TM,),
        in_specs=[pl.BlockSpec((TM, D), lambda i: (i, 0)),
                  pl.BlockSpec((1, D), lambda i: (0, 0)),
                  pl.BlockSpec((None, 1, D), lambda i: (_tile_group(i), 0, 0)),
                  pl.BlockSpec((None, 1, D), lambda i: (_tile_group(i), 0, 0)),
                  pl.BlockSpec((D, n), lambda i: (0, 0))],
        out_specs=pl.BlockSpec((TM, n), lambda i: (i, 0)),
        out_shape=jax.ShapeDtypeStruct((T, n), f32),
        compiler_params=_cparams(("parallel",)),
        name=name,
    )(x, g, shift, scale, w)


def _res_matmul_kernel(y_ref, w_ref, x_ref, gm_ref, o_ref):
    d = jnp.dot(y_ref[...], w_ref[...], preferred_element_type=f32)
    o_ref[...] = x_ref[...] + gm_ref[...] * d


def _res_matmul(y, w, x, gmod, name):
    k = y.shape[1]
    return pl.pallas_call(
        _res_matmul_kernel,
        grid=(T // TM,),
        in_specs=[pl.BlockSpec((TM, k), lambda i: (i, 0)),
                  pl.BlockSpec((k, D), lambda i: (0, 0)),
                  pl.BlockSpec((TM, D), lambda i: (i, 0)),
                  pl.BlockSpec((None, 1, D), lambda i: (_tile_group(i), 0, 0))],
        out_specs=pl.BlockSpec((TM, D), lambda i: (i, 0)),
        out_shape=jax.ShapeDtypeStruct((T, D), f32),
        compiler_params=_cparams(("parallel",)),
        name=name,
    )(y, w, x, gmod)


def _rg_kernel(gp_ref, xp_ref, cw_ref, cb_ref, w4_ref, b4_ref, lam_ref, h0_ref, y_ref, fin_ref,
               pad, a_f, b_f, a_b, b_b, h_f, h_b):
    L, tc = RG_SEQ, RG_TC
    is_prompt = pl.program_id(0) < NP // RG_SEQ
    far = jnp.where(is_prompt, 0, L)
    pad[0:SUBLANE, :] = jnp.zeros((SUBLANE, tc), f32)
    pad[SUBLANE + L:2 * SUBLANE + L, :] = jnp.zeros((SUBLANE, tc), f32)
    pad[SUBLANE:SUBLANE + L, :] = xp_ref[...]
    pos = lax.broadcasted_iota(jnp.int32, (L, tc), 0) & (PROMPT_LEN - 1)
    xr = jnp.zeros((L, tc), f32) + cb_ref[...]
    for j in range(CONV_W):
        off = j - CONV_LEFT
        xs = pad[pl.ds(SUBLANE + off, L), :]
        if off < 0:
            xs = jnp.where(pos + far >= -off, xs, 0.0)
        elif off > 0:
            xs = jnp.where(pos - far < PROMPT_LEN - off, xs, 0.0)
        xr = xr + xs * cw_ref[j:j + 1, :]
    pre = jnp.dot(xr.astype(bf16), w4_ref[...], preferred_element_type=f32) + b4_ref[...]
    sp = _softplus(-lam_ref[...])
    for d, (a_ref, b_ref, reset_pos) in enumerate(((a_f, b_f, 0), (a_b, b_b, PROMPT_LEN - 1))):
        r = _sigmoid(pre[:, (2 * d) * tc:(2 * d + 1) * tc])
        i = _sigmoid(pre[:, (2 * d + 1) * tc:(2 * d + 2) * tc])
        a = jnp.exp(-RG_C * r * sp[d:d + 1, :])
        b = jnp.sqrt(1.0 - a * a) * (i * xr)
        a_ref[...] = jnp.where(pos + far == reset_pos, 0.0, a)
        b_ref[...] = b

    def body(t, carry):
        hf, hb = carry
        base = pl.multiple_of(t * SUBLANE, SUBLANE)
        rbase = pl.multiple_of(L - SUBLANE - t * SUBLANE, SUBLANE)
        for k in range(SUBLANE):
            rf = base + k
            hf = a_f[pl.ds(rf, 1), :] * hf + b_f[pl.ds(rf, 1), :]
            h_f[pl.ds(rf, 1), :] = hf
            rb = rbase + (SUBLANE - 1 - k)
            hb = a_b[pl.ds(rb, 1), :] * hb + b_b[pl.ds(rb, 1), :]
            h_b[pl.ds(rb, 1), :] = hb
        return hf, hb

    h0 = h0_ref[...]
    lax.fori_loop(0, L // SUBLANE, body, (h0[0:1, :], h0[1:2, :]))
    y_ref[...] = (_gelu(gp_ref[...]) * (h_f[...] + h_b[...])).astype(bf16)
    for q in range(L // PROMPT_LEN):
        fin_ref[q:q + 1, :] = h_f[(q + 1) * PROMPT_LEN - 1:(q + 1) * PROMPT_LEN, :]
        fin_ref[4 + q:5 + q, :] = h_b[q * PROMPT_LEN:q * PROMPT_LEN + 1, :]


def _rg_scan(proj, conv_w, conv_b, w4, b4, lam, h0):
    ncb = D // RG_TC
    L = RG_SEQ
    return pl.pallas_call(
        _rg_kernel,
        grid=(T // L, ncb),
        in_specs=[pl.BlockSpec((L, RG_TC), lambda s, c: (s, c)),
                  pl.BlockSpec((L, RG_TC), lambda s, c: (s, ncb + c)),
                  pl.BlockSpec((CONV_W, RG_TC), lambda s, c: (0, c)),
                  pl.BlockSpec((1, RG_TC), lambda s, c: (0, c)),
                  pl.BlockSpec((None, RG_TC, 4 * RG_TC), lambda s, c: (c, 0, 0)),
                  pl.BlockSpec((None, 1, 4 * RG_TC), lambda s, c: (c, 0, 0)),
                  pl.BlockSpec((2, RG_TC), lambda s, c: (0, c)),
                  pl.BlockSpec((None, 2, RG_TC), lambda s, c: (s, 0, c))],
        out_specs=[pl.BlockSpec((L, RG_TC), lambda s, c: (s, c)),
                   pl.BlockSpec((None, SUBLANE, RG_TC), lambda s, c: (s, 0, c))],
        out_shape=[jax.ShapeDtypeStruct((T, D), bf16),
                   jax.ShapeDtypeStruct((T // L, SUBLANE, D), f32)],
        scratch_shapes=[pltpu.VMEM((L + 2 * SUBLANE, RG_TC), f32)] + [pltpu.VMEM((L, RG_TC), f32)] * 6,
        compiler_params=_cparams(("parallel", "parallel")),
        name="rg_scan",
    )(proj, proj, conv_w, conv_b, w4, b4, lam, h0)


def _rg_gate_weights(wa, ba, wi, bi):
    ncb = D // RG_TC
    per = RG_TC // RG_BW
    eye = jnp.eye(per, dtype=f32)

    def bd(w):
        w = w.reshape(ncb, per, RG_BW, RG_BW)
        return jnp.einsum('cpij,pq->cpiqj', w, eye).reshape(ncb, RG_TC, RG_TC)

    w4 = jnp.concatenate([bd(wa[0]), bd(wi[0]), bd(wa[1]), bd(wi[1])], axis=-1).astype(bf16)
    b4 = jnp.concatenate([v.reshape(ncb, 1, RG_TC) for v in (ba[0], bi[0], ba[1], bi[1])], axis=-1)
    return w4, b4


def _gla_kernel(*refs, L, has_s0):
    if has_s0:
        (q_ref, k_ref, v_ref, g_ref, z_ref, wal_ref, bal_ref, ng_ref, s0_ref,
         o_ref, sf_ref, la_f, la_b, o_f, o_b, s_f, s_b) = refs
    else:
        (q_ref, k_ref, v_ref, g_ref, z_ref, wal_ref, bal_ref, ng_ref,
         o_ref, sf_ref, la_f, la_b, o_f, o_b, s_f, s_b) = refs
    C = GLA_CHUNK
    n = L // C
    z = z_ref[...]
    for d, la_ref in enumerate((la_f, la_b)):
        pre = jnp.dot(z, wal_ref[d], precision=lax.Precision.HIGHEST, preferred_element_type=f32) + bal_ref[d]
        la_ref[...] = -_softplus(-pre) * (1.0 / GLA_TAU)
    ri = lax.broadcasted_iota(jnp.int32, (C, C), 0)
    ci = lax.broadcasted_iota(jnp.int32, (C, C), 1)

    for d, s_ref in enumerate((s_f, s_b)):
        s_ref[...] = s0_ref[d] if has_s0 else jnp.zeros((GLA_DK, GLA_DV), f32)

    def one_dir(d, c, la_ref, s_ref, o_dst):
        causal = (ri >= ci) if d == 0 else (ci >= ri)
        r0 = pl.multiple_of(c * C, C)
        la = la_ref[pl.ds(r0, C), :]
        bcum = jnp.dot(causal.astype(f32), la, precision=lax.Precision.HIGHEST, preferred_element_type=f32)
        edge = (C - SUBLANE) if d == 0 else 0
        tot8 = bcum[edge:edge + SUBLANE, :]
        tot_row = tot8[SUBLANE - 1:SUBLANE, :] if d == 0 else tot8[0:1, :]
        g_col = jnp.exp(tot8).T[:, (SUBLANE - 1 if d == 0 else 0):(SUBLANE if d == 0 else 1)]
        q = q_ref[pl.ds(r0, C), :] * (GLA_DK ** -0.5)
        k = k_ref[pl.ds(r0, C), :]
        v = v_ref[pl.ds(r0, C), :].astype(bf16)
        q_in = (q * jnp.exp(bcum)).astype(bf16)
        k_in = (k * jnp.exp(-bcum)).astype(bf16)
        k_st = (k * jnp.exp(tot_row - bcum)).astype(bf16)
        att = lax.dot_general(q_in, k_in, (((1,), (1,)), ((), ())), preferred_element_type=f32)
        att = jnp.where(causal, att, 0.0).astype(bf16)
        s = s_ref[...]
        o_dst[pl.ds(r0, C), :] = (jnp.dot(att, v, preferred_element_type=f32)
                                  + jnp.dot(q_in, s.astype(bf16), preferred_element_type=f32))
        u = lax.dot_general(k_st, v, (((0,), (0,)), ((), ())), preferred_element_type=f32)
        s_ref[...] = g_col * s + u

    def chunk(t, _):
        one_dir(0, t, la_f, s_f, o_f)
        one_dir(1, n - 1 - t, la_b, s_b, o_b)
        return 0

    lax.fori_loop(0, n, chunk, 0)
    sf_ref[0] = s_f[...]
    sf_ref[1] = s_b[...]
    o = o_f[...] + o_b[...]
    o = o * lax.rsqrt(jnp.mean(o * o, axis=-1, keepdims=True) + EPS) * ng_ref[...]
    g = g_ref[...]
    o_ref[...] = (g * _sigmoid(g) * o).astype(bf16)


def _gla_attn(proj, wal, bal, ng, s0, n_seq, L):
    has_s0 = s0 is not None
    H = GLA_HEADS
    in_specs = [pl.BlockSpec((L, GLA_DK), lambda b, h: (b, h)),
                pl.BlockSpec((L, GLA_DK), lambda b, h: (b, H + h)),
                pl.BlockSpec((L, GLA_DV), lambda b, h: (b, GLA_V // GLA_DV + h)),
                pl.BlockSpec((L, GLA_DV), lambda b, h: (b, 2 * GLA_V // GLA_DV + h)),
                pl.BlockSpec((L, LANE), lambda b, h: (b, (2 * GLA_QK + 2 * GLA_V) // LANE)),
                pl.BlockSpec((2, LANE, GLA_DK), lambda b, h: (0, 0, h)),
                pl.BlockSpec((2, 1, GLA_DK), lambda b, h: (0, 0, h)),
                pl.BlockSpec((1, GLA_DV), lambda b, h: (0, h))]
    args = [proj, proj, proj, proj, proj, wal, bal, ng]
    if has_s0:
        in_specs.append(pl.BlockSpec((None, 2, None, GLA_DK, GLA_DV), lambda b, h: (b, 0, h, 0, 0)))
        args.append(s0)
    return pl.pallas_call(
        functools.partial(_gla_kernel, L=L, has_s0=has_s0),
        grid=(n_seq, H),
        in_specs=in_specs,
        out_specs=[pl.BlockSpec((L, GLA_DV), lambda b, h: (b, h)),
                   pl.BlockSpec((None, 2, None, GLA_DK, GLA_DV), lambda b, h: (b, 0, h, 0, 0))],
        out_shape=[jax.ShapeDtypeStruct((n_seq * L, GLA_V), bf16),
                   jax.ShapeDtypeStruct((n_seq, 2, H, GLA_DK, GLA_DV), f32)],
        scratch_shapes=[pltpu.VMEM((L, GLA_DK), f32), pltpu.VMEM((L, GLA_DK), f32),
                        pltpu.VMEM((L, GLA_DV), f32), pltpu.VMEM((L, GLA_DV), f32),
                        pltpu.VMEM((GLA_DK, GLA_DV), f32), pltpu.VMEM((GLA_DK, GLA_DV), f32)],
        compiler_params=_cparams(("parallel", "parallel")),
        name="gla_attn_s0" if has_s0 else "gla_attn",
    )(*args)


def _top_vals(s, n):
    out = []
    cur = s
    for _ in range(n):
        m = jnp.max(cur, axis=0, keepdims=True)
        out.append(m)
        cur = jnp.where(cur == m, -jnp.inf, cur)
    return out


NOT_ROUTED = 64.0


def _top_vals_ranked(s, n):
    out = []
    cur = s
    rank = jnp.full(s.shape, NOT_ROUTED, f32)
    for k in range(n):
        m = jnp.max(cur, axis=0, keepdims=True)
        out.append(m)
        hit = cur == m
        rank = jnp.where(hit, float(k), rank)
        cur = jnp.where(hit, -jnp.inf, cur)
    return out, rank


def _peer_route_kernel(x_ref, g_ref, sh_ref, sc_ref, wqt_ref, k1_ref, k2_ref,
                       xmt_ref, cnt1_ref, e1_ref, rank2_ref, e2_ref, qt, cand):
    xm = _normmod(x_ref[...], g_ref[...], sh_ref[...], sc_ref[...])
    xmt = xm.T.astype(bf16)
    xmt_ref[...] = xmt
    qt[...] = jnp.dot(wqt_ref[...], xmt, preferred_element_type=f32)
    cand[...] = jnp.full((CAND_ROWS, TM), -jnp.inf, f32)

    def head(h, _):
        r0 = pl.multiple_of(h * 2 * PEER_HALF, 2 * PEER_HALF)
        q1 = qt[pl.ds(r0, PEER_HALF), :].astype(bf16)
        q2 = qt[pl.ds(r0 + PEER_HALF, PEER_HALF), :].astype(bf16)
        s1 = jnp.dot(k1_ref[h], q1, preferred_element_type=f32)
        s2 = jnp.dot(k2_ref[h], q2, preferred_element_type=f32)
        v1 = _top_vals(s1, PEER_TOPK)
        v2, rank2 = _top_vals_ranked(s2, PEER_TOPK)
        for idx, (a, b) in enumerate(CAND):
            cand[idx:idx + 1, :] = v1[a] + v2[b]
        m = _top_vals(cand[...], PEER_TOPK)
        z = jnp.zeros((1, TM), f32)
        for kk in range(PEER_TOPK):
            z = z + jnp.exp(m[kk] - m[0])
        cnt = [jnp.zeros((1, TM), f32) for _ in range(PEER_TOPK)]
        for idx, (a, b) in enumerate(CAND):
            cnt[a] = cnt[a] + jnp.where(cand[idx:idx + 1, :] >= m[PEER_TOPK - 1], 1.0, 0.0)
        cnt1 = jnp.zeros((PEER_NKEYS, TM), f32)
        for a in range(PEER_TOPK):
            cnt1 = jnp.where(s1 == v1[a], cnt[a], cnt1)
        cnt1_ref[h] = cnt1
        e1_ref[h] = jnp.exp(s1 - v1[0])
        rank2_ref[h] = rank2
        e2_ref[h] = jnp.exp(s2 - v2[0]) / z
        return 0

    lax.fori_loop(0, PEER_HEADS, head, 0)


def _peer_route(x, g, shift, scale, wqt, k1, k2):
    side = jax.ShapeDtypeStruct((PEER_HEADS, PEER_NKEYS, T), f32)
    side_spec = pl.BlockSpec((PEER_HEADS, PEER_NKEYS, TM), lambda i: (0, 0, i))
    return pl.pallas_call(
        _peer_route_kernel,
        grid=(T // TM,),
        in_specs=[pl.BlockSpec((TM, D), lambda i: (i, 0)),
                  pl.BlockSpec((1, D), lambda i: (0, 0)),
                  pl.BlockSpec((None, 1, D), lambda i: (_tile_group(i), 0, 0)),
                  pl.BlockSpec((None, 1, D), lambda i: (_tile_group(i), 0, 0)),
                  pl.BlockSpec((2 * PEER_HEADS * PEER_HALF, D), lambda i: (0, 0)),
                  pl.BlockSpec((PEER_HEADS, PEER_NKEYS, PEER_HALF), lambda i: (0, 0, 0)),
                  pl.BlockSpec((PEER_HEADS, PEER_NKEYS, PEER_HALF), lambda i: (0, 0, 0))],
        out_specs=[pl.BlockSpec((D, TM), lambda i: (0, i)), side_spec, side_spec, side_spec, side_spec],
        out_shape=[jax.ShapeDtypeStruct((D, T), bf16), side, side, side, side],
        scratch_shapes=[pltpu.VMEM((2 * PEER_HEADS * PEER_HALF, TM), f32), pltpu.VMEM((CAND_ROWS, TM), f32)],
        compiler_params=_cparams(("parallel",)),
        name="peer_route",
    )(x, g, shift, scale, wqt, k1, k2)


PEER_N2T = 32
PEER_SUB = 32
PEER_LC = LANE
PEER_NB = PEER_N // PEER_TE
PEER_UNITS = (T // PEER_TM) * PEER_NB
PEER_LAG = 2


def _zero_token(r):
    parts = [r[i:i + SUBLANE, j:j + PEER_LC] for i in range(0, r.shape[0], 2 * SUBLANE)
             for j in range(0, r.shape[1], 2 * LANE)]
    tok = jnp.minimum(jnp.abs(functools.reduce(jnp.add, parts)), 0.0)
    return jnp.concatenate([tok] * (PEER_SUB // SUBLANE), axis=0)


def _peer_dense_kernel(xmt_ref, u_ref, vt_ref, cnt1_ref, e1_ref, rank2_ref, e2_ref, x_ref, gm_ref, o_ref,
                       st0, st1, at0, at1, acc):
    f = pl.program_id(0)
    k = f - PEER_LAG

    @pl.when(f == 0)
    def _():
        for ref in (st0, st1, at0, at1):
            ref[...] = jnp.zeros_like(ref)

    @pl.when(jnp.logical_or(f == 0, k % PEER_NB == 0))
    def _():
        acc[...] = jnp.zeros_like(acc)

    n1_blocks = PEER_TE // PEER_NKEYS
    n_lc = PEER_TM // PEER_LC
    n_it = PEER_NKEYS // PEER_N2T
    e_rows = PEER_TE // n_it
    d_rows = D // n_it

    def phases(st_new, st_cur, at_cur, at_old):
        def tile(idx, _):
            rows_e = pl.ds(pl.multiple_of(idx * e_rows, e_rows), e_rows)
            rows_d = pl.ds(pl.multiple_of(idx * d_rows, d_rows), d_rows)
            r1 = jnp.dot(u_ref[rows_e, :], xmt_ref[...], preferred_element_type=f32)
            st_new[rows_e, :] = r1
            r2 = jnp.dot(vt_ref[rows_d, :], at_old[...], preferred_element_type=f32)
            acc[rows_d, :] += r2
            order_after = {n_lc - 2: _zero_token(r1), n_lc - 1: _zero_token(r2)}
            n2_0 = pl.multiple_of(idx * PEER_N2T, PEER_N2T)
            for lc in range(n_lc):
                lanes = slice(lc * PEER_LC, (lc + 1) * PEER_LC)
                init = order_after.get(lc, jnp.zeros((PEER_SUB, PEER_LC), f32)).astype(bf16)
                for sub in range(0, PEER_N2T, PEER_SUB):
                    gates = [init for _ in range(n1_blocks)]
                    for h in range(PEER_HEADS):
                        r2 = rank2_ref[h, pl.ds(n2_0 + sub, PEER_SUB), lanes].astype(bf16)
                        e2 = e2_ref[h, pl.ds(n2_0 + sub, PEER_SUB), lanes].astype(bf16)
                        for a in range(n1_blocks):
                            cnt = jnp.broadcast_to(cnt1_ref[h, a:a + 1, lanes], (PEER_SUB, PEER_LC)).astype(bf16)
                            e1 = jnp.broadcast_to(e1_ref[h, a:a + 1, lanes], (PEER_SUB, PEER_LC)).astype(bf16)
                            gates[a] = gates[a] + jnp.where(r2 < cnt, e2 * e1, jnp.zeros((), bf16))
                    for a in range(n1_blocks):
                        r0 = a * PEER_NKEYS + n2_0 + sub
                        s = st_cur[pl.ds(r0, PEER_SUB), lanes].astype(bf16)
                        at_cur[pl.ds(r0, PEER_SUB), lanes] = gates[a] * _gelu(s)
            return 0

        lax.fori_loop(0, n_it, tile, 0)

    @pl.when(f % 2 == 0)
    def _():
        phases(st0, st1, at1, at0)

    @pl.when(f % 2 == 1)
    def _():
        phases(st1, st0, at0, at1)

    @pl.when(jnp.logical_and(k >= 0, k % PEER_NB == PEER_NB - 1))
    def _():
        o_ref[...] = x_ref[...] + gm_ref[...] * acc[...].T


def _peer_tile_group(i):
    return jnp.where(i < NP // PEER_TM, 0, 1 + (i - NP // PEER_TM) // (SAMPLE_LEN // PEER_TM))


def _peer_dense(xmt, u16, vt16, cnt1, e1, rank2, e2, x, gmod):
    n1b = PEER_TE // PEER_NKEYS

    def blk(f, lag):
        return jnp.clip(f - lag, 0, PEER_UNITS - 1) % PEER_NB

    def tok(f, lag):
        return jnp.clip(f - lag, 0, PEER_UNITS - 1) // PEER_NB

    row_spec = pl.BlockSpec((PEER_HEADS, n1b, PEER_TM), lambda f: (0, blk(f, 1), tok(f, 1)))
    full_spec = pl.BlockSpec((PEER_HEADS, PEER_NKEYS, PEER_TM), lambda f: (0, 0, tok(f, 1)))
    return pl.pallas_call(
        _peer_dense_kernel,
        grid=(PEER_UNITS + PEER_LAG,),
        in_specs=[pl.BlockSpec((D, PEER_TM), lambda f: (0, tok(f, 0))),
                  pl.BlockSpec((PEER_TE, D), lambda f: (blk(f, 0), 0)),
                  pl.BlockSpec((D, PEER_TE), lambda f: (0, blk(f, PEER_LAG))),
                  row_spec, row_spec, full_spec, full_spec,
                  pl.BlockSpec((PEER_TM, D), lambda f: (tok(f, PEER_LAG), 0)),
                  pl.BlockSpec((None, 1, D), lambda f: (_peer_tile_group(tok(f, PEER_LAG)), 0, 0))],
        out_specs=pl.BlockSpec((PEER_TM, D), lambda f: (tok(f, PEER_LAG), 0)),
        out_shape=jax.ShapeDtypeStruct((T, D), f32),
        scratch_shapes=[pltpu.VMEM((PEER_TE, PEER_TM), f32), pltpu.VMEM((PEER_TE, PEER_TM), f32),
                        pltpu.VMEM((PEER_TE, PEER_TM), bf16), pltpu.VMEM((PEER_TE, PEER_TM), bf16),
                        pltpu.VMEM((D, PEER_TM), f32)],
        compiler_params=_cparams(("arbitrary",)),
        name="peer_dense",
    )(xmt, u16, vt16, cnt1, e1, rank2, e2, x, gmod)


def _final_norm_kernel(x_ref, g_ref, o_ref):
    x = x_ref[...]
    o_ref[...] = (x * lax.rsqrt(jnp.mean(x * x, axis=-1, keepdims=True) + EPS)) * g_ref[...]


def _final_norm(x, g):
    return pl.pallas_call(
        _final_norm_kernel,
        grid=(T // TM,),
        in_specs=[pl.BlockSpec((TM, D), lambda i: (i, 0)), pl.BlockSpec((1, D), lambda i: (0, 0))],
        out_specs=pl.BlockSpec((TM, D), lambda i: (i, 0)),
        out_shape=jax.ShapeDtypeStruct((T, D), f32),
        compiler_params=_cparams(("parallel",)),
        name="final_norm",
    )(x, g)


def _grid_pos_embed(rows, dim):
    t = jnp.arange(rows * GRID_W)
    r = (t // GRID_W).astype(f32)
    col = (t % GRID_W).astype(f32)
    nf = dim // 4
    freq = 1.0 / (10000.0 ** (jnp.arange(nf, dtype=f32) / nf))
    ar = r[:, None] * freq
    ac = col[:, None] * freq
    return jnp.concatenate([jnp.sin(ar), jnp.cos(ar), jnp.sin(ac), jnp.cos(ac)], axis=-1)


def _to_col_major(x, rows):
    b, n, d = x.shape
    return x.reshape(b, rows, GRID_W, d).swapaxes(1, 2).reshape(b, n, d)


def _from_col_major(x, rows):
    b, n, d = x.shape
    return x.reshape(b, GRID_W, rows, d).swapaxes(1, 2).reshape(b, n, d)


def kernel(x_prompt, x_sample, state_rglru, state_gla, c, c_ctx, norm1_g, norm2_g, ada_w, ada_b, peer_wq, peer_k1, peer_k2, peer_u, peer_v, rg_w_in, rg_conv_w, rg_conv_b, rg_wa, rg_ba, rg_wi, rg_bi, rg_lambda, rg_w_out, gla_w_in, gla_w_alpha, gla_b_alpha, gla_norm_g, gla_w_out, final_norm_g):
    rows = SAMPLE_LEN // GRID_W
    xs = x_sample + _grid_pos_embed(rows, D)[None]
    x = jnp.concatenate([x_prompt.reshape(NP, D), xs.reshape(NS, D)], axis=0)

    cond = jnp.concatenate([c_ctx[None], c, jnp.zeros((SUBLANE - N_GROUPS, D), f32)], axis=0)
    mods = _ada_all(cond.T, ada_w, ada_b).reshape(DEPTH, N_GROUPS, N_ADA, 1, D)

    rg_new, gla_new = [], []
    for l in range(DEPTH):
        j = l // 2
        mod = [mods[l, :, i] for i in range(N_ADA)]
        if l % 2 == 0:
            proj = _nm_matmul(x, norm1_g[l][None], mod[0], mod[1], rg_w_in[j].astype(bf16), "rg_in")
            w4, b4 = _rg_gate_weights(rg_wa[j], rg_ba[j], rg_wi[j], rg_bi[j])
            h0 = jnp.concatenate([jnp.zeros((NP // RG_SEQ, 2, D), f32), state_rglru[:, j]], axis=0)
            y, fin = _rg_scan(proj, rg_conv_w[j], rg_conv_b[j][None], w4, b4, rg_lambda[j], h0)
            st = fin[:NP // RG_SEQ].reshape(NP // RG_SEQ, 2, RG_SEQ // PROMPT_LEN, D)
            rg_new.append(st.transpose(0, 2, 1, 3).reshape(N_PROMPT_SEQ, 2, D))
            x = _res_matmul(y, rg_w_out[j].astype(bf16), x, mod[2], "rg_out")
        else:
            w_in = jnp.concatenate([gla_w_in[j], jnp.zeros((D, GLA_IN_PAD - GLA_IN), f32)], axis=1).astype(bf16)
            proj = _nm_matmul(x, norm1_g[l][None], mod[0], mod[1], w_in, "gla_in")
            wal = jnp.zeros((2, LANE, GLA_QK), f32)
            wal = wal.at[0, 0:GLA_RANK].set(gla_w_alpha[j, 0]).at[1, GLA_RANK:2 * GLA_RANK].set(gla_w_alpha[j, 1])
            bal = gla_b_alpha[j][:, None, :]
            ng = gla_norm_g[j][None]
            o_p, st = _gla_attn(proj, wal, bal, ng, None, N_PROMPT_SEQ, PROMPT_LEN)
            gla_new.append(st)
            proj_s = _to_col_major(proj[NP:].reshape(N_SAMPLE_SEQ, SAMPLE_LEN, GLA_IN_PAD), rows)
            o_s, _ = _gla_attn(proj_s.reshape(NS, GLA_IN_PAD), wal, bal, ng, state_gla[:, j], N_SAMPLE_SEQ, SAMPLE_LEN)
            o_s = _from_col_major(o_s.reshape(N_SAMPLE_SEQ, SAMPLE_LEN, GLA_V), rows).reshape(NS, GLA_V)
            y = jnp.concatenate([o_p[:NP], o_s], axis=0)
            x = _res_matmul(y, gla_w_out[j].astype(bf16), x, mod[2], "gla_out")
        xmt, cnt1, e1, rank2, e2 = _peer_route(x, norm2_g[l][None], mod[3], mod[4], peer_wq[l].T.astype(bf16),
                                             peer_k1[l].astype(bf16), peer_k2[l].astype(bf16))
        x = _peer_dense(xmt, peer_u[l].astype(bf16), peer_v[l].T.astype(bf16), cnt1, e1, rank2, e2, x, mod[5])

    y = _final_norm(x, final_norm_g[None])
    y_prompt = y[:NP].reshape(N_PROMPT_SEQ, PROMPT_LEN, D)
    y_sample = y[NP:].reshape(N_SAMPLE_SEQ, SAMPLE_LEN, D)
    return (y_prompt, y_sample, jnp.stack(rg_new, axis=1), jnp.stack(gla_new, axis=1))
```

```python
import functools
import math

import jax
import jax.numpy as jnp
from jax import lax
from jax.experimental import pallas as pl
from jax.experimental.pallas import tpu as pltpu

f32 = jnp.float32
bf16 = jnp.bfloat16

D = 1024
N_PROMPT_SEQ, PROMPT_LEN = 16, 256
N_SAMPLE_SEQ, SAMPLE_LEN = 2, 1024
NP = N_PROMPT_SEQ * PROMPT_LEN
NS = N_SAMPLE_SEQ * SAMPLE_LEN
T = NP + NS
DEPTH = 4
GRID_W = 64
N_ADA = 6
N_GROUPS = 1 + N_SAMPLE_SEQ
EPS = 1e-6
CONV_W, CONV_LEFT = 4, 2
RG_BLOCKS, RG_BW = 16, 64
RG_C = 8.0
GLA_HEADS, GLA_DK, GLA_DV = 4, 128, 256
GLA_QK, GLA_V = 512, 1024
GLA_RANK = 16
GLA_TAU = 16.0
GLA_CHUNK = 64
GLA_IN = 2 * GLA_QK + 2 * GLA_V + 2 * GLA_RANK
GLA_IN_PAD = 2 * GLA_QK + 2 * GLA_V + 128
PEER_HEADS, PEER_NKEYS, PEER_HALF, PEER_TOPK = 8, 128, 128, 16
PEER_N = PEER_NKEYS * PEER_NKEYS

LANE = 128
SUBLANE = 8
VMEM_LIMIT = 52 * 1024 * 1024

TM = 256
RG_SEQ = 1024
RG_TC = 512
PEER_TM = 512
PEER_TE = 1024
CAND = [(a, b) for a in range(PEER_TOPK) for b in range(PEER_TOPK) if (a + 1) * (b + 1) <= PEER_TOPK]
CAND_ROWS = -(-len(CAND) // SUBLANE) * SUBLANE


def _cparams(sem, flags=None):
    return pltpu.CompilerParams(dimension_semantics=sem, vmem_limit_bytes=VMEM_LIMIT, flags=flags)


def _tile_group(i):
    return jnp.where(i < NP // TM, 0, 1 + (i - NP // TM) // (SAMPLE_LEN // TM))


def _gelu(x):
    return 0.5 * x * (1.0 + jnp.tanh(0.7978845608028654 * (x + 0.044715 * (x * x * x))))


def _sigmoid(x):
    return 1.0 / (1.0 + jnp.exp(-x))


def _softplus(x):
    return jnp.maximum(x, 0.0) + jnp.log1p(jnp.exp(-jnp.abs(x)))


def _normmod(x, g, shift, scale):
    ms = jnp.mean(x * x, axis=-1, keepdims=True)
    y = (x * lax.rsqrt(ms + EPS)) * g
    return y * (1.0 + scale) + shift


def _ada_kernel(ct_ref, w_ref, b_ref, o_ref):
    ct = ct_ref[...]
    s = ct * _sigmoid(ct)
    w = w_ref[...]
    rows = [jnp.sum(w * s[:, r:r + 1], axis=0, keepdims=True) for r in range(N_GROUPS)]
    o_ref[...] = jnp.concatenate(rows, axis=0) + b_ref[...]


def _ada_all(cond_t, ada_w, ada_b):
    tn = 1536
    nd = N_ADA * D
    return pl.pallas_call(
        _ada_kernel,
        grid=(DEPTH, nd // tn),
        in_specs=[pl.BlockSpec((D, SUBLANE), lambda l, j: (0, 0)),
                  pl.BlockSpec((None, D, tn), lambda l, j: (l, 0, j)),
                  pl.BlockSpec((None, 1, tn), lambda l, j: (l, 0, j))],
        out_specs=pl.BlockSpec((None, N_GROUPS, tn), lambda l, j: (l, 0, j)),
        out_shape=jax.ShapeDtypeStruct((DEPTH, N_GROUPS, nd), f32),
        compiler_params=_cparams(("parallel", "parallel")),
        name="ada_mod",
    )(cond_t, ada_w, ada_b.reshape(DEPTH, 1, nd))


def _nm_matmul_kernel(x_ref, g_ref, sh_ref, sc_ref, w_ref, o_ref):
    xm = _normmod(x_ref[...], g_ref[...], sh_ref[...], sc_ref[...])
    o_ref[...] = jnp.dot(xm.astype(bf16), w_ref[...], preferred_element_type=f32)


def _nm_matmul(x, g, shift, scale, w, name):
    n = w.shape[1]
    return pl.pallas_call(
        _nm_matmul_kernel,
        grid=(T // TM,),
        in_specs=[pl.BlockSpec((TM, D), lambda i: (i, 0)),
                  pl.BlockSpec((1, D), lambda i: (0, 0)),
                  pl.BlockSpec((None, 1, D), lambda i: (_tile_group(i), 0, 0)),
                  pl.BlockSpec((None, 1, D), lambda i: (_tile_group(i), 0, 0)),
                  pl.BlockSpec((D, n), lambda i: (0, 0))],
        out_specs=pl.BlockSpec((TM, n), lambda i: (i, 0)),
        out_shape=jax.ShapeDtypeStruct((T, n), f32),
        compiler_params=_cparams(("parallel",)),
        name=name,
    )(x, g, shift, scale, w)


def _res_matmul_kernel(y_ref, w_ref, x_ref, gm_ref, o_ref):
    d = jnp.dot(y_ref[...], w_ref[...], preferred_element_type=f32)
    o_ref[...] = x_ref[...] + gm_ref[...] * d


def _res_matmul(y, w, x, gmod, name):
    k = y.shape[1]
    return pl.pallas_call(
        _res_matmul_kernel,
        grid=(T // TM,),
        in_specs=[pl.BlockSpec((TM, k), lambda i: (i, 0)),
                  pl.BlockSpec((k, D), lambda i: (0, 0)),
                  pl.BlockSpec((TM, D), lambda i: (i, 0)),
                  pl.BlockSpec((None, 1, D), lambda i: (_tile_group(i), 0, 0))],
        out_specs=pl.BlockSpec((TM, D), lambda i: (i, 0)),
        out_shape=jax.ShapeDtypeStruct((T, D), f32),
        compiler_params=_cparams(("parallel",)),
        name=name,
    )(y, w, x, gmod)


def _rg_kernel(gp_ref, xp_ref, cw_ref, cb_ref, w4_ref, b4_ref, lam_ref, h0_ref, y_ref, fin_ref,
               pad, a_f, b_f, a_b, b_b, h_f, h_b):
    L, tc = RG_SEQ, RG_TC
    is_prompt = pl.program_id(0) < NP // RG_SEQ
    far = jnp.where(is_prompt, 0, L)
    pad[0:SUBLANE, :] = jnp.zeros((SUBLANE, tc), f32)
    pad[SUBLANE + L:2 * SUBLANE + L, :] = jnp.zeros((SUBLANE, tc), f32)
    pad[SUBLANE:SUBLANE + L, :] = xp_ref[...]
    pos = lax.broadcasted_iota(jnp.int32, (L, tc), 0) & (PROMPT_LEN - 1)
    xr = jnp.zeros((L, tc), f32) + cb_ref[...]
    for j in range(CONV_W):
        off = j - CONV_LEFT
        xs = pad[pl.ds(SUBLANE + off, L), :]
        if off < 0:
            xs = jnp.where(pos + far >= -off, xs, 0.0)
        elif off > 0:
            xs = jnp.where(pos - far < PROMPT_LEN - off, xs, 0.0)
        xr = xr + xs * cw_ref[j:j + 1, :]
    pre = jnp.dot(xr.astype(bf16), w4_ref[...], preferred_element_type=f32) + b4_ref[...]
    sp = _softplus(-lam_ref[...])
    for d, (a_ref, b_ref, reset_pos) in enumerate(((a_f, b_f, 0), (a_b, b_b, PROMPT_LEN - 1))):
        r = _sigmoid(pre[:, (2 * d) * tc:(2 * d + 1) * tc])
        i = _sigmoid(pre[:, (2 * d + 1) * tc:(2 * d + 2) * tc])
        a = jnp.exp(-RG_C * r * sp[d:d + 1, :])
        b = jnp.sqrt(1.0 - a * a) * (i * xr)
        a_ref[...] = jnp.where(pos + far == reset_pos, 0.0, a)
        b_ref[...] = b

    def body(t, carry):
        hf, hb = carry
        base = pl.multiple_of(t * SUBLANE, SUBLANE)
        rbase = pl.multiple_of(L - SUBLANE - t * SUBLANE, SUBLANE)
        for k in range(SUBLANE):
            rf = base + k
            hf = a_f[pl.ds(rf, 1), :] * hf + b_f[pl.ds(rf, 1), :]
            h_f[pl.ds(rf, 1), :] = hf
            rb = rbase + (SUBLANE - 1 - k)
            hb = a_b[pl.ds(rb, 1), :] * hb + b_b[pl.ds(rb, 1), :]
            h_b[pl.ds(rb, 1), :] = hb
        return hf, hb

    h0 = h0_ref[...]
    lax.fori_loop(0, L // SUBLANE, body, (h0[0:1, :], h0[1:2, :]))
    y_ref[...] = (_gelu(gp_ref[...]) * (h_f[...] + h_b[...])).astype(bf16)
    for q in range(L // PROMPT_LEN):
        fin_ref[q:q + 1, :] = h_f[(q + 1) * PROMPT_LEN - 1:(q + 1) * PROMPT_LEN, :]
        fin_ref[4 + q:5 + q, :] = h_b[q * PROMPT_LEN:q * PROMPT_LEN + 1, :]


def _rg_scan(proj, conv_w, conv_b, w4, b4, lam, h0):
    ncb = D // RG_TC
    L = RG_SEQ
    return pl.pallas_call(
        _rg_kernel,
        grid=(T // L, ncb),
        in_specs=[pl.BlockSpec((L, RG_TC), lambda s, c: (s, c)),
                  pl.BlockSpec((L, RG_TC), lambda s, c: (s, ncb + c)),
                  pl.BlockSpec((CONV_W, RG_TC), lambda s, c: (0, c)),
                  pl.BlockSpec((1, RG_TC), lambda s, c: (0, c)),
                  pl.BlockSpec((None, RG_TC, 4 * RG_TC), lambda s, c: (c, 0, 0)),
                  pl.BlockSpec((None, 1, 4 * RG_TC), lambda s, c: (c, 0, 0)),
                  pl.BlockSpec((2, RG_TC), lambda s, c: (0, c)),
                  pl.BlockSpec((None, 2, RG_TC), lambda s, c: (s, 0, c))],
        out_specs=[pl.BlockSpec((L, RG_TC), lambda s, c: (s, c)),
                   pl.BlockSpec((None, SUBLANE, RG_TC), lambda s, c: (s, 0, c))],
        out_shape=[jax.ShapeDtypeStruct((T, D), bf16),
                   jax.ShapeDtypeStruct((T // L, SUBLANE, D), f32)],
        scratch_shapes=[pltpu.VMEM((L + 2 * SUBLANE, RG_TC), f32)] + [pltpu.VMEM((L, RG_TC), f32)] * 6,
        compiler_params=_cparams(("parallel", "parallel")),
        name="rg_scan",
    )(proj, proj, conv_w, conv_b, w4, b4, lam, h0)


def _rg_gate_weights(wa, ba, wi, bi):
    ncb = D // RG_TC
    per = RG_TC // RG_BW
    eye = jnp.eye(per, dtype=f32)

    def bd(w):
        w = w.reshape(ncb, per, RG_BW, RG_BW)
        return jnp.einsum('cpij,pq->cpiqj', w, eye).reshape(ncb, RG_TC, RG_TC)

    w4 = jnp.concatenate([bd(wa[0]), bd(wi[0]), bd(wa[1]), bd(wi[1])], axis=-1).astype(bf16)
    b4 = jnp.concatenate([v.reshape(ncb, 1, RG_TC) for v in (ba[0], bi[0], ba[1], bi[1])], axis=-1)
    return w4, b4


GLA_PREP_UNROLL = 4


def _gla_kernel(*refs, L, has_s0):
    if has_s0:
        (q_ref, k_ref, v_ref, g_ref, z_ref, wal_ref, bal_ref, ng_ref, s0_ref, o_ref, sf_ref,
         la_f, la_b, o_f, o_b, qin_f, qin_b, u_f, u_b, gb_f, gb_b, s_f, s_b) = refs
    else:
        (q_ref, k_ref, v_ref, g_ref, z_ref, wal_ref, bal_ref, ng_ref, o_ref, sf_ref,
         la_f, la_b, o_f, o_b, qin_f, qin_b, u_f, u_b, gb_f, gb_b, s_f, s_b) = refs
    C = GLA_CHUNK
    n = L // C
    z = z_ref[...]
    for d, la_ref in enumerate((la_f, la_b)):
        pre = jnp.dot(z, wal_ref[d], precision=lax.Precision.HIGHEST, preferred_element_type=f32) + bal_ref[d]
        la_ref[...] = -_softplus(-pre) * (1.0 / GLA_TAU)
    ri = lax.broadcasted_iota(jnp.int32, (C, C), 0)
    ci = lax.broadcasted_iota(jnp.int32, (C, C), 1)

    for d, s_ref in enumerate((s_f, s_b)):
        s_ref[...] = s0_ref[d] if has_s0 else jnp.zeros((GLA_DK, GLA_DV), f32)

    def prepare(d, c, la_ref, qin_ref, u_ref, gb_ref, o_dst):
        causal = (ri >= ci) if d == 0 else (ci >= ri)
        r0 = pl.multiple_of(c * C, C)
        la = la_ref[pl.ds(r0, C), :]
        bcum = jnp.dot(causal.astype(f32), la, precision=lax.Precision.HIGHEST, preferred_element_type=f32)
        edge = (C - SUBLANE) if d == 0 else 0
        tot8 = bcum[edge:edge + SUBLANE, :]
        tot_row = tot8[SUBLANE - 1:SUBLANE, :] if d == 0 else tot8[0:1, :]
        g_col = jnp.exp(tot8).T[:, (SUBLANE - 1 if d == 0 else 0):(SUBLANE if d == 0 else 1)]
        gb_ref[c] = jnp.broadcast_to(g_col, (GLA_DK, LANE))
        q = q_ref[pl.ds(r0, C), :] * (GLA_DK ** -0.5)
        k = k_ref[pl.ds(r0, C), :]
        v = v_ref[pl.ds(r0, C), :].astype(bf16)
        q_in = (q * jnp.exp(bcum)).astype(bf16)
        qin_ref[pl.ds(r0, C), :] = q_in
        k_in = (k * jnp.exp(-bcum)).astype(bf16)
        k_st = (k * jnp.exp(tot_row - bcum)).astype(bf16)
        att = lax.dot_general(q_in, k_in, (((1,), (1,)), ((), ())), preferred_element_type=f32)
        att = jnp.where(causal, att, 0.0).astype(bf16)
        o_dst[pl.ds(r0, C), :] = jnp.dot(att, v, preferred_element_type=f32)
        u_ref[c] = lax.dot_general(k_st, v, (((0,), (0,)), ((), ())), preferred_element_type=f32)

    def prep_trip(t, _):
        for j in range(GLA_PREP_UNROLL):
            c = t * GLA_PREP_UNROLL + j
            prepare(0, c, la_f, qin_f, u_f, gb_f, o_f)
            prepare(1, c, la_b, qin_b, u_b, gb_b, o_b)
        return 0

    lax.fori_loop(0, n // GLA_PREP_UNROLL, prep_trip, 0)

    def scan_trip(t, _):
        for c, qin_ref, u_ref, gb_ref, s_ref, o_dst in ((t, qin_f, u_f, gb_f, s_f, o_f),
                                                        (n - 1 - t, qin_b, u_b, gb_b, s_b, o_b)):
            r0 = pl.multiple_of(c * C, C)
            s = s_ref[...]
            o_dst[pl.ds(r0, C), :] += jnp.dot(qin_ref[pl.ds(r0, C), :], s.astype(bf16), preferred_element_type=f32)
            g = gb_ref[c]
            s_ref[...] = jnp.concatenate([g] * (GLA_DV // LANE), axis=1) * s + u_ref[c]
        return 0

    lax.fori_loop(0, n, scan_trip, 0)
    sf_ref[0] = s_f[...]
    sf_ref[1] = s_b[...]
    o = o_f[...] + o_b[...]
    o = o * lax.rsqrt(jnp.mean(o * o, axis=-1, keepdims=True) + EPS) * ng_ref[...]
    g = g_ref[...]
    o_ref[...] = (g * _sigmoid(g) * o).astype(bf16)


def _gla_attn(proj, wal, bal, ng, s0, n_seq, L, row0):
    has_s0 = s0 is not None
    H = GLA_HEADS
    off = row0 // L
    n = L // GLA_CHUNK
    in_specs = [pl.BlockSpec((L, GLA_DK), lambda b, h: (off + b, h)),
                pl.BlockSpec((L, GLA_DK), lambda b, h: (off + b, H + h)),
                pl.BlockSpec((L, GLA_DV), lambda b, h: (off + b, GLA_V // GLA_DV + h)),
                pl.BlockSpec((L, GLA_DV), lambda b, h: (off + b, 2 * GLA_V // GLA_DV + h)),
                pl.BlockSpec((L, LANE), lambda b, h: (off + b, (2 * GLA_QK + 2 * GLA_V) // LANE)),
                pl.BlockSpec((2, LANE, GLA_DK), lambda b, h: (0, 0, h)),
                pl.BlockSpec((2, 1, GLA_DK), lambda b, h: (0, 0, h)),
                pl.BlockSpec((1, GLA_DV), lambda b, h: (0, h))]
    args = [proj, proj, proj, proj, proj, wal, bal, ng]
    if has_s0:
        in_specs.append(pl.BlockSpec((None, 2, None, GLA_DK, GLA_DV), lambda b, h: (b, 0, h, 0, 0)))
        args.append(s0)
    per_dir = [pltpu.VMEM((L, GLA_DK), f32), pltpu.VMEM((L, GLA_DV), f32), pltpu.VMEM((L, GLA_DK), bf16),
               pltpu.VMEM((n, GLA_DK, GLA_DV), f32), pltpu.VMEM((n, GLA_DK, LANE), f32),
               pltpu.VMEM((GLA_DK, GLA_DV), f32)]
    return pl.pallas_call(
        functools.partial(_gla_kernel, L=L, has_s0=has_s0),
        grid=(n_seq, H),
        in_specs=in_specs,
        out_specs=[pl.BlockSpec((L, GLA_DV), lambda b, h: (b, h)),
                   pl.BlockSpec((None, 2, None, GLA_DK, GLA_DV), lambda b, h: (b, 0, h, 0, 0))],
        out_shape=[jax.ShapeDtypeStruct((n_seq * L, GLA_V), bf16),
                   jax.ShapeDtypeStruct((n_seq, 2, H, GLA_DK, GLA_DV), f32)],
        scratch_shapes=[sc for kind in per_dir for sc in (kind, kind)],
        compiler_params=_cparams(("parallel", "parallel")),
        name="gla_attn_s0" if has_s0 else "gla_attn",
    )(*args)


def _top_vals(s, n):
    out = []
    cur = s
    for _ in range(n):
        m = jnp.max(cur, axis=0, keepdims=True)
        out.append(m)
        cur = jnp.where(cur == m, -jnp.inf, cur)
    return out


NOT_ROUTED = 64.0


def _top_vals_ranked(s, n):
    out = []
    cur = s
    rank = jnp.full(s.shape, NOT_ROUTED, f32)
    for k in range(n):
        m = jnp.max(cur, axis=0, keepdims=True)
        out.append(m)
        hit = cur == m
        rank = jnp.where(hit, float(k), rank)
        cur = jnp.where(hit, -jnp.inf, cur)
    return out, rank


def _top_exact(s, n):
    rows = lax.broadcasted_iota(jnp.int32, s.shape, 0)
    vals = []
    cur = s
    rank = jnp.full(s.shape, NOT_ROUTED, f32)
    for k in range(n):
        m = jnp.max(cur, axis=0, keepdims=True)
        first = jnp.min(jnp.where(cur == m, rows, s.shape[0]), axis=0, keepdims=True)
        one = rows == first
        vals.append(m)
        rank = jnp.where(one, float(k), rank)
        cur = jnp.where(one, -jnp.inf, cur)
    return vals, rank


def _count_ge(s, thr):
    return jnp.sum(jnp.where(s >= thr, 1.0, 0.0), axis=0, keepdims=True)


def _peer_route_kernel(x_ref, g_ref, sh_ref, sc_ref, wqt_ref, k1_ref, k2_ref,
                       xmt_ref, cnt1_ref, e1_ref, rank2_ref, e2_ref, qt, cand_a, cand_b):
    xm = _normmod(x_ref[...], g_ref[...], sh_ref[...], sc_ref[...])
    xmt = xm.T.astype(bf16)
    xmt_ref[...] = xmt
    qt[...] = jnp.dot(wqt_ref[...], xmt, preferred_element_type=f32)
    for cand in (cand_a, cand_b):
        cand[...] = jnp.full((CAND_ROWS, TM), -jnp.inf, f32)

    def scores(h):
        r0 = pl.multiple_of(h * 2 * PEER_HALF, 2 * PEER_HALF)
        q1 = qt[pl.ds(r0, PEER_HALF), :].astype(bf16)
        q2 = qt[pl.ds(r0 + PEER_HALF, PEER_HALF), :].astype(bf16)
        s1 = jnp.dot(k1_ref[h], q1, preferred_element_type=f32)
        s2 = jnp.dot(k2_ref[h], q2, preferred_element_type=f32)
        return s1, s2

    def softmax_norm(m):
        z = jnp.zeros((1, TM), f32)
        for kk in range(PEER_TOPK):
            z = z + jnp.exp(m[kk] - m[0])
        return z

    def head(h, cand):
        s1, s2 = scores(h)
        v1 = _top_vals(s1, PEER_TOPK)
        v2, rank2 = _top_vals_ranked(s2, PEER_TOPK)
        for idx, (a, b) in enumerate(CAND):
            cand[idx:idx + 1, :] = v1[a] + v2[b]
        m = _top_vals(cand[...], PEER_TOPK)
        z = softmax_norm(m)
        cnt = [jnp.zeros((1, TM), f32) for _ in range(PEER_TOPK)]
        for idx, (a, b) in enumerate(CAND):
            cnt[a] = cnt[a] + jnp.where(cand[idx:idx + 1, :] >= m[PEER_TOPK - 1], 1.0, 0.0)
        cnt1 = jnp.zeros((PEER_NKEYS, TM), f32)
        for a in range(PEER_TOPK):
            cnt1 = jnp.where(s1 == v1[a], cnt[a], cnt1)
        cnt1_ref[h] = cnt1
        e1_ref[h] = jnp.exp(s1 - v1[0])
        rank2_ref[h] = rank2
        e2_ref[h] = jnp.exp(s2 - v2[0]) / z
        k = float(PEER_TOPK)
        tied = jnp.logical_or(jnp.logical_or(_count_ge(s1, v1[PEER_TOPK - 1]) != k,
                                             _count_ge(s2, v2[PEER_TOPK - 1]) != k),
                              _count_ge(cand[...], m[PEER_TOPK - 1]) != k)
        return jnp.max(jnp.where(tied, 1.0, 0.0)) > 0.0

    def head_exact(h, cand):
        s1, s2 = scores(h)
        v1, rank1 = _top_exact(s1, PEER_TOPK)
        v2, rank2 = _top_exact(s2, PEER_TOPK)
        for idx, (a, b) in enumerate(CAND):
            cand[idx:idx + 1, :] = v1[a] + v2[b]
        m, rank_c = _top_exact(cand[...], PEER_TOPK)
        z = softmax_norm(m)
        cnt = [jnp.zeros((1, TM), f32) for _ in range(PEER_TOPK)]
        for idx, (a, b) in enumerate(CAND):
            cnt[a] = cnt[a] + jnp.where(rank_c[idx:idx + 1, :] < float(PEER_TOPK), 1.0, 0.0)
        cnt1 = jnp.zeros((PEER_NKEYS, TM), f32)
        for a in range(PEER_TOPK):
            cnt1 = jnp.where(rank1 == float(a), cnt[a], cnt1)
        cnt1_ref[h] = cnt1
        e1_ref[h] = jnp.exp(s1 - v1[0])
        rank2_ref[h] = rank2
        e2_ref[h] = jnp.exp(s2 - v2[0]) / z

    def head_pair(hh, _):
        tied_a = head(2 * hh, cand_a)
        tied_b = head(2 * hh + 1, cand_b)

        @pl.when(tied_a)
        def _():
            head_exact(2 * hh, cand_a)

        @pl.when(tied_b)
        def _():
            head_exact(2 * hh + 1, cand_b)

        return 0

    lax.fori_loop(0, PEER_HEADS // 2, head_pair, 0)


def _peer_route(x, g, shift, scale, wqt, k1, k2):
    side = jax.ShapeDtypeStruct((PEER_HEADS, PEER_NKEYS, T), f32)
    side_spec = pl.BlockSpec((PEER_HEADS, PEER_NKEYS, TM), lambda i: (0, 0, i))
    return pl.pallas_call(
        _peer_route_kernel,
        grid=(T // TM,),
        in_specs=[pl.BlockSpec((TM, D), lambda i: (i, 0)),
                  pl.BlockSpec((1, D), lambda i: (0, 0)),
                  pl.BlockSpec((None, 1, D), lambda i: (_tile_group(i), 0, 0)),
                  pl.BlockSpec((None, 1, D), lambda i: (_tile_group(i), 0, 0)),
                  pl.BlockSpec((2 * PEER_HEADS * PEER_HALF, D), lambda i: (0, 0)),
                  pl.BlockSpec((PEER_HEADS, PEER_NKEYS, PEER_HALF), lambda i: (0, 0, 0)),
                  pl.BlockSpec((PEER_HEADS, PEER_NKEYS, PEER_HALF), lambda i: (0, 0, 0))],
        out_specs=[pl.BlockSpec((D, TM), lambda i: (0, i)), side_spec, side_spec, side_spec, side_spec],
        out_shape=[jax.ShapeDtypeStruct((D, T), bf16), side, side, side, side],
        scratch_shapes=[pltpu.VMEM((2 * PEER_HEADS * PEER_HALF, TM), f32), pltpu.VMEM((CAND_ROWS, TM), f32),
                        pltpu.VMEM((CAND_ROWS, TM), f32)],
        compiler_params=_cparams(("parallel",)),
        name="peer_route",
    )(x, g, shift, scale, wqt, k1, k2)


PEER_N2T = 32
PEER_SUB = 32
PEER_LC = LANE
PEER_NB = PEER_N // PEER_TE
PEER_UNITS = (T // PEER_TM) * PEER_NB
PEER_LAG = 2


def _zero_token(r):
    parts = [r[i:i + SUBLANE, j:j + PEER_LC] for i in range(0, r.shape[0], 2 * SUBLANE)
             for j in range(0, r.shape[1], 2 * LANE)]
    tok = jnp.minimum(jnp.abs(functools.reduce(jnp.add, parts)), 0.0)
    return jnp.concatenate([tok] * (PEER_SUB // SUBLANE), axis=0)


def _peer_dense_kernel(xmt_ref, u_ref, vt_ref, cnt1_ref, e1_ref, rank2_ref, e2_ref, x_ref, gm_ref, o_ref,
                       st, at, acc):
    f = pl.program_id(0)
    k = f - PEER_LAG

    @pl.when(f == 0)
    def _():
        for ref in (st, at):
            ref[...] = jnp.zeros_like(ref)

    @pl.when(jnp.logical_or(f == 0, k % PEER_NB == 0))
    def _():
        acc[...] = jnp.zeros_like(acc)

    n1_blocks = PEER_TE // PEER_NKEYS
    n_lc = PEER_TM // PEER_LC
    n_it = PEER_NKEYS // PEER_N2T
    e_rows = PEER_TE // n_it
    d_rows = D // n_it

    def phases(st_new, st_cur, at_cur, at_old):
        def tile(idx, _):
            rows_e = pl.ds(pl.multiple_of(idx * e_rows, e_rows), e_rows)
            rows_d = pl.ds(pl.multiple_of(idx * d_rows, d_rows), d_rows)
            r1 = jnp.dot(u_ref[rows_e, :].astype(bf16), xmt_ref[...], preferred_element_type=f32)
            st_new[rows_e, :] = r1
            r2 = jnp.dot(vt_ref[rows_d, :], at_old[...], preferred_element_type=f32)
            acc[rows_d, :] += r2
            order_after = {n_lc - 2: _zero_token(r1), n_lc - 1: _zero_token(r2)}
            n2_0 = pl.multiple_of(idx * PEER_N2T, PEER_N2T)
            for lc in range(n_lc):
                lanes = slice(lc * PEER_LC, (lc + 1) * PEER_LC)
                init = order_after.get(lc, jnp.zeros((PEER_SUB, PEER_LC), f32)).astype(bf16)
                for sub in range(0, PEER_N2T, PEER_SUB):
                    gates = [init for _ in range(n1_blocks)]
                    for h in range(PEER_HEADS):
                        r2 = rank2_ref[h, pl.ds(n2_0 + sub, PEER_SUB), lanes].astype(bf16)
                        e2 = e2_ref[h, pl.ds(n2_0 + sub, PEER_SUB), lanes].astype(bf16)
                        for a in range(n1_blocks):
                            cnt = jnp.broadcast_to(cnt1_ref[h, a:a + 1, lanes], (PEER_SUB, PEER_LC)).astype(bf16)
                            e1 = jnp.broadcast_to(e1_ref[h, a:a + 1, lanes], (PEER_SUB, PEER_LC)).astype(bf16)
                            gates[a] = gates[a] + jnp.where(r2 < cnt, e2 * e1, jnp.zeros((), bf16))
                    for a in range(n1_blocks):
                        r0 = a * PEER_NKEYS + n2_0 + sub
                        s = st_cur[pl.ds(r0, PEER_SUB), lanes].astype(bf16)
                        at_cur[pl.ds(r0, PEER_SUB), lanes] = gates[a] * _gelu(s)
            return 0

        lax.fori_loop(0, n_it, tile, 0)

    w = f % 2
    phases(st.at[w], st.at[1 - w], at.at[1 - w], at.at[w])

    @pl.when(jnp.logical_and(k >= 0, k % PEER_NB == PEER_NB - 1))
    def _():
        o_ref[...] = x_ref[...] + gm_ref[...] * acc[...].T


def _peer_tile_group(i):
    return jnp.where(i < NP // PEER_TM, 0, 1 + (i - NP // PEER_TM) // (SAMPLE_LEN // PEER_TM))


def _peer_dense(xmt, u_tab, vt16, cnt1, e1, rank2, e2, x, gmod):
    n1b = PEER_TE // PEER_NKEYS

    def blk(f, lag):
        return jnp.clip(f - lag, 0, PEER_UNITS - 1) % PEER_NB

    def tok(f, lag):
        return jnp.clip(f - lag, 0, PEER_UNITS - 1) // PEER_NB

    row_spec = pl.BlockSpec((PEER_HEADS, n1b, PEER_TM), lambda f: (0, blk(f, 1), tok(f, 1)))
    full_spec = pl.BlockSpec((PEER_HEADS, PEER_NKEYS, PEER_TM), lambda f: (0, 0, tok(f, 1)))
    return pl.pallas_call(
        _peer_dense_kernel,
        grid=(PEER_UNITS + PEER_LAG,),
        in_specs=[pl.BlockSpec((D, PEER_TM), lambda f: (0, tok(f, 0))),
                  pl.BlockSpec((PEER_TE, D), lambda f: (blk(f, 0), 0)),
                  pl.BlockSpec((D, PEER_TE), lambda f: (0, blk(f, PEER_LAG))),
                  row_spec, row_spec, full_spec, full_spec,
                  pl.BlockSpec((PEER_TM, D), lambda f: (tok(f, PEER_LAG), 0)),
                  pl.BlockSpec((None, 1, D), lambda f: (_peer_tile_group(tok(f, PEER_LAG)), 0, 0))],
        out_specs=pl.BlockSpec((PEER_TM, D), lambda f: (tok(f, PEER_LAG), 0)),
        out_shape=jax.ShapeDtypeStruct((T, D), f32),
        scratch_shapes=[pltpu.VMEM((2, PEER_TE, PEER_TM), f32), pltpu.VMEM((2, PEER_TE, PEER_TM), bf16),
                        pltpu.VMEM((D, PEER_TM), f32)],
        compiler_params=_cparams(("arbitrary",)),
        name="peer_dense",
    )(xmt, u_tab, vt16, cnt1, e1, rank2, e2, x, gmod)


def _final_norm_kernel(x_ref, g_ref, o_ref):
    x = x_ref[...]
    o_ref[...] = (x * lax.rsqrt(jnp.mean(x * x, axis=-1, keepdims=True) + EPS)) * g_ref[...]


def _final_norm(x, g):
    return pl.pallas_call(
        _final_norm_kernel,
        grid=(T // TM,),
        in_specs=[pl.BlockSpec((TM, D), lambda i: (i, 0)), pl.BlockSpec((1, D), lambda i: (0, 0))],
        out_specs=pl.BlockSpec((TM, D), lambda i: (i, 0)),
        out_shape=jax.ShapeDtypeStruct((T, D), f32),
        compiler_params=_cparams(("parallel",)),
        name="final_norm",
    )(x, g)


def _grid_pos_embed(rows, dim):
    t = jnp.arange(rows * GRID_W)
    r = (t // GRID_W).astype(f32)
    col = (t % GRID_W).astype(f32)
    nf = dim // 4
    freq = 1.0 / (10000.0 ** (jnp.arange(nf, dtype=f32) / nf))
    ar = r[:, None] * freq
    ac = col[:, None] * freq
    return jnp.concatenate([jnp.sin(ar), jnp.cos(ar), jnp.sin(ac), jnp.cos(ac)], axis=-1)


def _to_col_major(x, rows):
    b, n, d = x.shape
    return x.reshape(b, rows, GRID_W, d).swapaxes(1, 2).reshape(b, n, d)


def _from_col_major(x, rows):
    b, n, d = x.shape
    return x.reshape(b, GRID_W, rows, d).swapaxes(1, 2).reshape(b, n, d)


def kernel(x_prompt, x_sample, state_rglru, state_gla, c, c_ctx, norm1_g, norm2_g, ada_w, ada_b, peer_wq, peer_k1, peer_k2, peer_u, peer_v, rg_w_in, rg_conv_w, rg_conv_b, rg_wa, rg_ba, rg_wi, rg_bi, rg_lambda, rg_w_out, gla_w_in, gla_w_alpha, gla_b_alpha, gla_norm_g, gla_w_out, final_norm_g):
    rows = SAMPLE_LEN // GRID_W
    xs = x_sample + _grid_pos_embed(rows, D)[None]
    x = jnp.concatenate([x_prompt.reshape(NP, D), xs.reshape(NS, D)], axis=0)

    cond = jnp.concatenate([c_ctx[None], c, jnp.zeros((SUBLANE - N_GROUPS, D), f32)], axis=0)
    mods = _ada_all(cond.T, ada_w, ada_b).reshape(DEPTH, N_GROUPS, N_ADA, 1, D)

    rg_new, gla_new = [], []
    for l in range(DEPTH):
        j = l // 2
        mod = [mods[l, :, i] for i in range(N_ADA)]
        if l % 2 == 0:
            proj = _nm_matmul(x, norm1_g[l][None], mod[0], mod[1], rg_w_in[j].astype(bf16), "rg_in")
            w4, b4 = _rg_gate_weights(rg_wa[j], rg_ba[j], rg_wi[j], rg_bi[j])
            h0 = jnp.concatenate([jnp.zeros((NP // RG_SEQ, 2, D), f32), state_rglru[:, j]], axis=0)
            y, fin = _rg_scan(proj, rg_conv_w[j], rg_conv_b[j][None], w4, b4, rg_lambda[j], h0)
            st = fin[:NP // RG_SEQ].reshape(NP // RG_SEQ, 2, RG_SEQ // PROMPT_LEN, D)
            rg_new.append(st.transpose(0, 2, 1, 3).reshape(N_PROMPT_SEQ, 2, D))
            x = _res_matmul(y, rg_w_out[j].astype(bf16), x, mod[2], "rg_out")
        else:
            w_in = jnp.concatenate([gla_w_in[j], jnp.zeros((D, GLA_IN_PAD - GLA_IN), f32)], axis=1).astype(bf16)
            proj = _nm_matmul(x, norm1_g[l][None], mod[0], mod[1], w_in, "gla_in")
            wal = jnp.zeros((2, LANE, GLA_QK), f32)
            wal = wal.at[0, 0:GLA_RANK].set(gla_w_alpha[j, 0]).at[1, GLA_RANK:2 * GLA_RANK].set(gla_w_alpha[j, 1])
            bal = gla_b_alpha[j][:, None, :]
            ng = gla_norm_g[j][None]
            o_p, st = _gla_attn(proj, wal, bal, ng, None, N_PROMPT_SEQ, PROMPT_LEN, 0)
            gla_new.append(st)
            proj_s = _to_col_major(proj[NP:].reshape(N_SAMPLE_SEQ, SAMPLE_LEN, GLA_IN_PAD), rows).reshape(NS, GLA_IN_PAD)
            o_s, _ = _gla_attn(proj_s, wal, bal, ng, state_gla[:, j], N_SAMPLE_SEQ, SAMPLE_LEN, 0)
            o_s = _from_col_major(o_s.reshape(N_SAMPLE_SEQ, SAMPLE_LEN, GLA_V), rows).reshape(NS, GLA_V)
            y = jnp.concatenate([o_p, o_s], axis=0)
            x = _res_matmul(y, gla_w_out[j].astype(bf16), x, mod[2], "gla_out")
        xmt, cnt1, e1, rank2, e2 = _peer_route(x, norm2_g[l][None], mod[3], mod[4], peer_wq[l].T.astype(bf16),
                                             peer_k1[l].astype(bf16), peer_k2[l].astype(bf16))
        x = _peer_dense(xmt, peer_u[l], peer_v[l].T.astype(bf16), cnt1, e1, rank2, e2, x, mod[5])

    y = _final_norm(x, final_norm_g[None])
    y_prompt = y[:NP].reshape(N_PROMPT_SEQ, PROMPT_LEN, D)
    y_sample = y[NP:].reshape(N_SAMPLE_SEQ, SAMPLE_LEN, D)
    return (y_prompt, y_sample, jnp.stack(rg_new, axis=1), jnp.stack(gla_new, axis=1))
```

```python
import functools
import math

import jax
import jax.numpy as jnp
from jax import lax
from jax.experimental import pallas as pl
from jax.experimental.pallas import tpu as pltpu

f32 = jnp.float32
bf16 = jnp.bfloat16

D = 1024
N_PROMPT_SEQ, PROMPT_LEN = 16, 256
N_SAMPLE_SEQ, SAMPLE_LEN = 2, 1024
NP = N_PROMPT_SEQ * PROMPT_LEN
NS = N_SAMPLE_SEQ * SAMPLE_LEN
T = NP + NS
DEPTH = 4
GRID_W = 64
N_ADA = 6
N_GROUPS = 1 + N_SAMPLE_SEQ
EPS = 1e-6
CONV_W, CONV_LEFT = 4, 2
RG_BLOCKS, RG_BW = 16, 64
RG_C = 8.0
GLA_HEADS, GLA_DK, GLA_DV = 4, 128, 256
GLA_QK, GLA_V = 512, 1024
GLA_RANK = 16
GLA_TAU = 16.0
GLA_CHUNK = 64
GLA_IN = 2 * GLA_QK + 2 * GLA_V + 2 * GLA_RANK
GLA_IN_PAD = 2 * GLA_QK + 2 * GLA_V + 128
PEER_HEADS, PEER_NKEYS, PEER_HALF, PEER_TOPK = 8, 128, 128, 16
PEER_N = PEER_NKEYS * PEER_NKEYS

LANE = 128
SUBLANE = 8
VMEM_LIMIT = 52 * 1024 * 1024

TM = 256
RG_SEQ = 1024
RG_TC = 512
PEER_TM = 512
PEER_TE = 1024
CAND = [(a, b) for a in range(PEER_TOPK) for b in range(PEER_TOPK) if (a + 1) * (b + 1) <= PEER_TOPK]
CAND_ROWS = -(-len(CAND) // SUBLANE) * SUBLANE


def _cparams(sem, flags=None):
    return pltpu.CompilerParams(dimension_semantics=sem, vmem_limit_bytes=VMEM_LIMIT, flags=flags)


def _tile_group(i):
    return jnp.where(i < NP // TM, 0, 1 + (i - NP // TM) // (SAMPLE_LEN // TM))


def _gelu(x):
    return 0.5 * x * (1.0 + jnp.tanh(0.7978845608028654 * (x + 0.044715 * (x * x * x))))


def _sigmoid(x):
    return 1.0 / (1.0 + jnp.exp(-x))


def _softplus(x):
    return jnp.maximum(x, 0.0) + jnp.log1p(jnp.exp(-jnp.abs(x)))


def _normmod(x, g, shift, scale):
    ms = jnp.mean(x * x, axis=-1, keepdims=True)
    y = (x * lax.rsqrt(ms + EPS)) * g
    return y * (1.0 + scale) + shift


def _ada_kernel(ct_ref, w_ref, b_ref, o_ref):
    ct = ct_ref[...]
    s = ct * _sigmoid(ct)
    w = w_ref[...]
    rows = [jnp.sum(w * s[:, r:r + 1], axis=0, keepdims=True) for r in range(N_GROUPS)]
    o_ref[...] = jnp.concatenate(rows, axis=0) + b_ref[...]


def _ada_all(cond_t, ada_w, ada_b):
    tn = 1536
    nd = N_ADA * D
    return pl.pallas_call(
        _ada_kernel,
        grid=(DEPTH, nd // tn),
        in_specs=[pl.BlockSpec((D, SUBLANE), lambda l, j: (0, 0)),
                  pl.BlockSpec((None, D, tn), lambda l, j: (l, 0, j)),
                  pl.BlockSpec((None, 1, tn), lambda l, j: (l, 0, j))],
        out_specs=pl.BlockSpec((None, N_GROUPS, tn), lambda l, j: (l, 0, j)),
        out_shape=jax.ShapeDtypeStruct((DEPTH, N_GROUPS, nd), f32),
        compiler_params=_cparams(("parallel", "parallel")),
        name="ada_mod",
    )(cond_t, ada_w, ada_b.reshape(DEPTH, 1, nd))


def _nm_matmul_kernel(x_ref, g_ref, sh_ref, sc_ref, w_ref, o_ref):
    xm = _normmod(x_ref[...], g_ref[...], sh_ref[...], sc_ref[...])
    o_ref[...] = jnp.dot(xm.astype(bf16), w_ref[...], preferred_element_type=f32)


def _nm_matmul(x, g, shift, scale, w, name):
    n = w.shape[1]
    return pl.pallas_call(
        _nm_matmul_kernel,
        grid=(T // TM,),
        in_specs=[pl.BlockSpec((TM, D), lambda i: (i, 0)),
                  pl.BlockSpec((1, D), lambda i: (0, 0)),
                  pl.BlockSpec((None, 1, D), lambda i: (_tile_group(i), 0, 0)),
                  pl.BlockSpec((None, 1, D), lambda i: (_tile_group(i), 0, 0)),
                  pl.BlockSpec((D, n), lambda i: (0, 0))],
        out_specs=pl.BlockSpec((TM, n), lambda i: (i, 0)),
        out_shape=jax.ShapeDtypeStruct((T, n), f32),
        compiler_params=_cparams(("parallel",)),
        name=name,
    )(x, g, shift, scale, w)


def _res_matmul_kernel(y_ref, w_ref, x_ref, gm_ref, o_ref):
    d = jnp.dot(y_ref[...], w_ref[...], preferred_element_type=f32)
    o_ref[...] = x_ref[...] + gm_ref[...] * d


def _res_matmul(y, w, x, gmod, name):
    k = y.shape[1]
    return pl.pallas_call(
        _res_matmul_kernel,
        grid=(T // TM,),
        in_specs=[pl.BlockSpec((TM, k), lambda i: (i, 0)),
                  pl.BlockSpec((k, D), lambda i: (0, 0)),
                  pl.BlockSpec((TM, D), lambda i: (i, 0)),
                  pl.BlockSpec((None, 1, D), lambda i: (_tile_group(i), 0, 0))],
        out_specs=pl.BlockSpec((TM, D), lambda i: (i, 0)),
        out_shape=jax.ShapeDtypeStruct((T, D), f32),
        compiler_params=_cparams(("parallel",)),
        name=name,
    )(y, w, x, gmod)


def _rg_kernel(gp_ref, xp_ref, cw_ref, cb_ref, w4_ref, b4_ref, lam_ref, h0_ref, y_ref, fin_ref,
               pad, a_f, b_f, a_b, b_b, h_f, h_b):
    L, tc = RG_SEQ, RG_TC
    is_prompt = pl.program_id(0) < NP // RG_SEQ
    far = jnp.where(is_prompt, 0, L)
    pad[0:SUBLANE, :] = jnp.zeros((SUBLANE, tc), f32)
    pad[SUBLANE + L:2 * SUBLANE + L, :] = jnp.zeros((SUBLANE, tc), f32)
    pad[SUBLANE:SUBLANE + L, :] = xp_ref[...]
    pos = lax.broadcasted_iota(jnp.int32, (L, tc), 0) & (PROMPT_LEN - 1)
    xr = jnp.zeros((L, tc), f32) + cb_ref[...]
    for j in range(CONV_W):
        off = j - CONV_LEFT
        xs = pad[pl.ds(SUBLANE + off, L), :]
        if off < 0:
            xs = jnp.where(pos + far >= -off, xs, 0.0)
        elif off > 0:
            xs = jnp.where(pos - far < PROMPT_LEN - off, xs, 0.0)
        xr = xr + xs * cw_ref[j:j + 1, :]
    pre = jnp.dot(xr.astype(bf16), w4_ref[...], preferred_element_type=f32) + b4_ref[...]
    sp = _softplus(-lam_ref[...])
    for d, (a_ref, b_ref, reset_pos) in enumerate(((a_f, b_f, 0), (a_b, b_b, PROMPT_LEN - 1))):
        r = _sigmoid(pre[:, (2 * d) * tc:(2 * d + 1) * tc])
        i = _sigmoid(pre[:, (2 * d + 1) * tc:(2 * d + 2) * tc])
        a = jnp.exp(-RG_C * r * sp[d:d + 1, :])
        b = jnp.sqrt(1.0 - a * a) * (i * xr)
        a_ref[...] = jnp.where(pos + far == reset_pos, 0.0, a)
        b_ref[...] = b

    def body(t, carry):
        hf, hb = carry
        base = pl.multiple_of(t * SUBLANE, SUBLANE)
        rbase = pl.multiple_of(L - SUBLANE - t * SUBLANE, SUBLANE)
        for k in range(SUBLANE):
            rf = base + k
            hf = a_f[pl.ds(rf, 1), :] * hf + b_f[pl.ds(rf, 1), :]
            h_f[pl.ds(rf, 1), :] = hf
            rb = rbase + (SUBLANE - 1 - k)
            hb = a_b[pl.ds(rb, 1), :] * hb + b_b[pl.ds(rb, 1), :]
            h_b[pl.ds(rb, 1), :] = hb
        return hf, hb

    h0 = h0_ref[...]
    lax.fori_loop(0, L // SUBLANE, body, (h0[0:1, :], h0[1:2, :]))
    y_ref[...] = (_gelu(gp_ref[...]) * (h_f[...] + h_b[...])).astype(bf16)
    for q in range(L // PROMPT_LEN):
        fin_ref[q:q + 1, :] = h_f[(q + 1) * PROMPT_LEN - 1:(q + 1) * PROMPT_LEN, :]
        fin_ref[4 + q:5 + q, :] = h_b[q * PROMPT_LEN:q * PROMPT_LEN + 1, :]


def _rg_scan(proj, conv_w, conv_b, w4, b4, lam, h0):
    ncb = D // RG_TC
    L = RG_SEQ
    return pl.pallas_call(
        _rg_kernel,
        grid=(T // L, ncb),
        in_specs=[pl.BlockSpec((L, RG_TC), lambda s, c: (s, c)),
                  pl.BlockSpec((L, RG_TC), lambda s, c: (s, ncb + c)),
                  pl.BlockSpec((CONV_W, RG_TC), lambda s, c: (0, c)),
                  pl.BlockSpec((1, RG_TC), lambda s, c: (0, c)),
                  pl.BlockSpec((None, RG_TC, 4 * RG_TC), lambda s, c: (c, 0, 0)),
                  pl.BlockSpec((None, 1, 4 * RG_TC), lambda s, c: (c, 0, 0)),
                  pl.BlockSpec((2, RG_TC), lambda s, c: (0, c)),
                  pl.BlockSpec((None, 2, RG_TC), lambda s, c: (s, 0, c))],
        out_specs=[pl.BlockSpec((L, RG_TC), lambda s, c: (s, c)),
                   pl.BlockSpec((None, SUBLANE, RG_TC), lambda s, c: (s, 0, c))],
        out_shape=[jax.ShapeDtypeStruct((T, D), bf16),
                   jax.ShapeDtypeStruct((T // L, SUBLANE, D), f32)],
        scratch_shapes=[pltpu.VMEM((L + 2 * SUBLANE, RG_TC), f32)] + [pltpu.VMEM((L, RG_TC), f32)] * 6,
        compiler_params=_cparams(("parallel", "parallel")),
        name="rg_scan",
    )(proj, proj, conv_w, conv_b, w4, b4, lam, h0)


def _rg_gate_weights(wa, ba, wi, bi):
    ncb = D // RG_TC
    per = RG_TC // RG_BW
    eye = jnp.eye(per, dtype=f32)

    def bd(w):
        w = w.reshape(ncb, per, RG_BW, RG_BW)
        return jnp.einsum('cpij,pq->cpiqj', w, eye).reshape(ncb, RG_TC, RG_TC)

    w4 = jnp.concatenate([bd(wa[0]), bd(wi[0]), bd(wa[1]), bd(wi[1])], axis=-1).astype(bf16)
    b4 = jnp.concatenate([v.reshape(ncb, 1, RG_TC) for v in (ba[0], bi[0], ba[1], bi[1])], axis=-1)
    return w4, b4


GLA_PREP_UNROLL = 4


def _gla_kernel(*refs, L, has_s0):
    if has_s0:
        (q_ref, k_ref, v_ref, g_ref, z_ref, wal_ref, bal_ref, ng_ref, s0_ref, o_ref, sf_ref,
         la_f, la_b, o_f, o_b, qin_f, qin_b, u_f, u_b, gb_f, gb_b, s_f, s_b) = refs
    else:
        (q_ref, k_ref, v_ref, g_ref, z_ref, wal_ref, bal_ref, ng_ref, o_ref, sf_ref,
         la_f, la_b, o_f, o_b, qin_f, qin_b, u_f, u_b, gb_f, gb_b, s_f, s_b) = refs
    C = GLA_CHUNK
    n = L // C
    z = z_ref[...]
    for d, la_ref in enumerate((la_f, la_b)):
        pre = jnp.dot(z, wal_ref[d], precision=lax.Precision.HIGHEST, preferred_element_type=f32) + bal_ref[d]
        la_ref[...] = -_softplus(-pre) * (1.0 / GLA_TAU)
    ri = lax.broadcasted_iota(jnp.int32, (C, C), 0)
    ci = lax.broadcasted_iota(jnp.int32, (C, C), 1)

    for d, s_ref in enumerate((s_f, s_b)):
        s_ref[...] = s0_ref[d] if has_s0 else jnp.zeros((GLA_DK, GLA_DV), f32)

    def prepare(d, c, la_ref, qin_ref, u_ref, gb_ref, o_dst):
        causal = (ri >= ci) if d == 0 else (ci >= ri)
        r0 = pl.multiple_of(c * C, C)
        la = la_ref[pl.ds(r0, C), :]
        bcum = jnp.dot(causal.astype(f32), la, precision=lax.Precision.HIGHEST, preferred_element_type=f32)
        edge = (C - SUBLANE) if d == 0 else 0
        tot8 = bcum[edge:edge + SUBLANE, :]
        tot_row = tot8[SUBLANE - 1:SUBLANE, :] if d == 0 else tot8[0:1, :]
        g_col = jnp.exp(tot8).T[:, (SUBLANE - 1 if d == 0 else 0):(SUBLANE if d == 0 else 1)]
        gb_ref[c] = jnp.broadcast_to(g_col, (GLA_DK, LANE))
        q = q_ref[pl.ds(r0, C), :] * (GLA_DK ** -0.5)
        k = k_ref[pl.ds(r0, C), :]
        v = v_ref[pl.ds(r0, C), :].astype(bf16)
        q_in = (q * jnp.exp(bcum)).astype(bf16)
        qin_ref[pl.ds(r0, C), :] = q_in
        k_in = (k * jnp.exp(-bcum)).astype(bf16)
        k_st = (k * jnp.exp(tot_row - bcum)).astype(bf16)
        att = lax.dot_general(q_in, k_in, (((1,), (1,)), ((), ())), preferred_element_type=f32)
        att = jnp.where(causal, att, 0.0).astype(bf16)
        o_dst[pl.ds(r0, C), :] = jnp.dot(att, v, preferred_element_type=f32)
        u_ref[c] = lax.dot_general(k_st, v, (((0,), (0,)), ((), ())), preferred_element_type=f32)

    def prep_trip(t, _):
        for j in range(GLA_PREP_UNROLL):
            c = t * GLA_PREP_UNROLL + j
            prepare(0, c, la_f, qin_f, u_f, gb_f, o_f)
            prepare(1, c, la_b, qin_b, u_b, gb_b, o_b)
        return 0

    lax.fori_loop(0, n // GLA_PREP_UNROLL, prep_trip, 0)

    def scan_trip(t, _):
        for c, qin_ref, u_ref, gb_ref, s_ref, o_dst in ((t, qin_f, u_f, gb_f, s_f, o_f),
                                                        (n - 1 - t, qin_b, u_b, gb_b, s_b, o_b)):
            r0 = pl.multiple_of(c * C, C)
            s = s_ref[...]
            o_dst[pl.ds(r0, C), :] += jnp.dot(qin_ref[pl.ds(r0, C), :], s.astype(bf16), preferred_element_type=f32)
            g = gb_ref[c]
            s_ref[...] = jnp.concatenate([g] * (GLA_DV // LANE), axis=1) * s + u_ref[c]
        return 0

    lax.fori_loop(0, n, scan_trip, 0)
    sf_ref[0] = s_f[...]
    sf_ref[1] = s_b[...]
    o = o_f[...] + o_b[...]
    o = o * lax.rsqrt(jnp.mean(o * o, axis=-1, keepdims=True) + EPS) * ng_ref[...]
    g = g_ref[...]
    o_ref[...] = (g * _sigmoid(g) * o).astype(bf16)


def _gla_attn(proj, wal, bal, ng, s0, n_seq, L, row0):
    has_s0 = s0 is not None
    H = GLA_HEADS
    off = row0 // L
    n = L // GLA_CHUNK
    in_specs = [pl.BlockSpec((L, GLA_DK), lambda b, h: (off + b, h)),
                pl.BlockSpec((L, GLA_DK), lambda b, h: (off + b, H + h)),
                pl.BlockSpec((L, GLA_DV), lambda b, h: (off + b, GLA_V // GLA_DV + h)),
                pl.BlockSpec((L, GLA_DV), lambda b, h: (off + b, 2 * GLA_V // GLA_DV + h)),
                pl.BlockSpec((L, LANE), lambda b, h: (off + b, (2 * GLA_QK + 2 * GLA_V) // LANE)),
                pl.BlockSpec((2, LANE, GLA_DK), lambda b, h: (0, 0, h)),
                pl.BlockSpec((2, 1, GLA_DK), lambda b, h: (0, 0, h)),
                pl.BlockSpec((1, GLA_DV), lambda b, h: (0, h))]
    args = [proj, proj, proj, proj, proj, wal, bal, ng]
    if has_s0:
        in_specs.append(pl.BlockSpec((None, 2, None, GLA_DK, GLA_DV), lambda b, h: (b, 0, h, 0, 0)))
        args.append(s0)
    per_dir = [pltpu.VMEM((L, GLA_DK), f32), pltpu.VMEM((L, GLA_DV), f32), pltpu.VMEM((L, GLA_DK), bf16),
               pltpu.VMEM((n, GLA_DK, GLA_DV), f32), pltpu.VMEM((n, GLA_DK, LANE), f32),
               pltpu.VMEM((GLA_DK, GLA_DV), f32)]
    return pl.pallas_call(
        functools.partial(_gla_kernel, L=L, has_s0=has_s0),
        grid=(n_seq, H),
        in_specs=in_specs,
        out_specs=[pl.BlockSpec((L, GLA_DV), lambda b, h: (b, h)),
                   pl.BlockSpec((None, 2, None, GLA_DK, GLA_DV), lambda b, h: (b, 0, h, 0, 0))],
        out_shape=[jax.ShapeDtypeStruct((n_seq * L, GLA_V), bf16),
                   jax.ShapeDtypeStruct((n_seq, 2, H, GLA_DK, GLA_DV), f32)],
        scratch_shapes=[sc for kind in per_dir for sc in (kind, kind)],
        compiler_params=_cparams(("parallel", "parallel")),
        name="gla_attn_s0" if has_s0 else "gla_attn",
    )(*args)


def _top_vals(s, n):
    out = []
    cur = s
    for _ in range(n):
        m = jnp.max(cur, axis=0, keepdims=True)
        out.append(m)
        cur = jnp.where(cur == m, -jnp.inf, cur)
    return out


NOT_ROUTED = 64.0


def _top_vals_ranked(s, n):
    out = []
    cur = s
    rank = jnp.full(s.shape, NOT_ROUTED, f32)
    for k in range(n):
        m = jnp.max(cur, axis=0, keepdims=True)
        out.append(m)
        hit = cur == m
        rank = jnp.where(hit, float(k), rank)
        cur = jnp.where(hit, -jnp.inf, cur)
    return out, rank


def _top_exact(s, n):
    rows = lax.broadcasted_iota(jnp.int32, s.shape, 0)
    vals = []
    cur = s
    rank = jnp.full(s.shape, NOT_ROUTED, f32)
    for k in range(n):
        m = jnp.max(cur, axis=0, keepdims=True)
        first = jnp.min(jnp.where(cur == m, rows, s.shape[0]), axis=0, keepdims=True)
        one = rows == first
        vals.append(m)
        rank = jnp.where(one, float(k), rank)
        cur = jnp.where(one, -jnp.inf, cur)
    return vals, rank


def _count_ge(s, thr):
    return jnp.sum(jnp.where(s >= thr, 1.0, 0.0), axis=0, keepdims=True)


def _peer_route_kernel(x_ref, g_ref, sh_ref, sc_ref, wqt_ref, k1_ref, k2_ref,
                       xmt_ref, cnt1_ref, e1_ref, rank2_ref, e2_ref, qt, cand_a, cand_b):
    xm = _normmod(x_ref[...], g_ref[...], sh_ref[...], sc_ref[...])
    xmt = xm.T.astype(bf16)
    xmt_ref[...] = xmt
    qt[...] = jnp.dot(wqt_ref[...], xmt, preferred_element_type=f32)
    for cand in (cand_a, cand_b):
        cand[...] = jnp.full((CAND_ROWS, TM), -jnp.inf, f32)

    def scores(h):
        r0 = pl.multiple_of(h * 2 * PEER_HALF, 2 * PEER_HALF)
        q1 = qt[pl.ds(r0, PEER_HALF), :].astype(bf16)
        q2 = qt[pl.ds(r0 + PEER_HALF, PEER_HALF), :].astype(bf16)
        s1 = jnp.dot(k1_ref[h], q1, preferred_element_type=f32)
        s2 = jnp.dot(k2_ref[h], q2, preferred_element_type=f32)
        return s1, s2

    def softmax_norm(m):
        z = jnp.zeros((1, TM), f32)
        for kk in range(PEER_TOPK):
            z = z + jnp.exp(m[kk] - m[0])
        return z

    def head(h, cand):
        s1, s2 = scores(h)
        v1 = _top_vals(s1, PEER_TOPK)
        v2, rank2 = _top_vals_ranked(s2, PEER_TOPK)
        for idx, (a, b) in enumerate(CAND):
            cand[idx:idx + 1, :] = v1[a] + v2[b]
        m = _top_vals(cand[...], PEER_TOPK)
        z = softmax_norm(m)
        cnt = [jnp.zeros((1, TM), f32) for _ in range(PEER_TOPK)]
        for idx, (a, b) in enumerate(CAND):
            cnt[a] = cnt[a] + jnp.where(cand[idx:idx + 1, :] >= m[PEER_TOPK - 1], 1.0, 0.0)
        cnt1 = jnp.zeros((PEER_NKEYS, TM), f32)
        for a in range(PEER_TOPK):
            cnt1 = jnp.where(s1 == v1[a], cnt[a], cnt1)
        cnt1_ref[h] = cnt1
        e1_ref[h] = jnp.exp(s1 - v1[0])
        rank2_ref[h] = rank2
        e2_ref[h] = jnp.exp(s2 - v2[0]) / z
        k = float(PEER_TOPK)
        tied = jnp.logical_or(jnp.logical_or(_count_ge(s1, v1[PEER_TOPK - 1]) != k,
                                             _count_ge(s2, v2[PEER_TOPK - 1]) != k),
                              _count_ge(cand[...], m[PEER_TOPK - 1]) != k)
        return jnp.max(jnp.where(tied, 1.0, 0.0)) > 0.0

    def head_exact(h, cand):
        s1, s2 = scores(h)
        v1, rank1 = _top_exact(s1, PEER_TOPK)
        v2, rank2 = _top_exact(s2, PEER_TOPK)
        for idx, (a, b) in enumerate(CAND):
            cand[idx:idx + 1, :] = v1[a] + v2[b]
        m, rank_c = _top_exact(cand[...], PEER_TOPK)
        z = softmax_norm(m)
        cnt = [jnp.zeros((1, TM), f32) for _ in range(PEER_TOPK)]
        for idx, (a, b) in enumerate(CAND):
            cnt[a] = cnt[a] + jnp.where(rank_c[idx:idx + 1, :] < float(PEER_TOPK), 1.0, 0.0)
        cnt1 = jnp.zeros((PEER_NKEYS, TM), f32)
        for a in range(PEER_TOPK):
            cnt1 = jnp.where(rank1 == float(a), cnt[a], cnt1)
        cnt1_ref[h] = cnt1
        e1_ref[h] = jnp.exp(s1 - v1[0])
        rank2_ref[h] = rank2
        e2_ref[h] = jnp.exp(s2 - v2[0]) / z

    def head_pair(hh, _):
        tied_a = head(2 * hh, cand_a)
        tied_b = head(2 * hh + 1, cand_b)

        @pl.when(tied_a)
        def _():
            head_exact(2 * hh, cand_a)

        @pl.when(tied_b)
        def _():
            head_exact(2 * hh + 1, cand_b)

        return 0

    lax.fori_loop(0, PEER_HEADS // 2, head_pair, 0)


def _peer_route(x, g, shift, scale, wqt, k1, k2):
    side = jax.ShapeDtypeStruct((PEER_HEADS, PEER_NKEYS, T), f32)
    side_spec = pl.BlockSpec((PEER_HEADS, PEER_NKEYS, TM), lambda i: (0, 0, i))
    return pl.pallas_call(
        _peer_route_kernel,
        grid=(T // TM,),
        in_specs=[pl.BlockSpec((TM, D), lambda i: (i, 0)),
                  pl.BlockSpec((1, D), lambda i: (0, 0)),
                  pl.BlockSpec((None, 1, D), lambda i: (_tile_group(i), 0, 0)),
                  pl.BlockSpec((None, 1, D), lambda i: (_tile_group(i), 0, 0)),
                  pl.BlockSpec((2 * PEER_HEADS * PEER_HALF, D), lambda i: (0, 0)),
                  pl.BlockSpec((PEER_HEADS, PEER_NKEYS, PEER_HALF), lambda i: (0, 0, 0)),
                  pl.BlockSpec((PEER_HEADS, PEER_NKEYS, PEER_HALF), lambda i: (0, 0, 0))],
        out_specs=[pl.BlockSpec((D, TM), lambda i: (0, i)), side_spec, side_spec, side_spec, side_spec],
        out_shape=[jax.ShapeDtypeStruct((D, T), bf16), side, side, side, side],
        scratch_shapes=[pltpu.VMEM((2 * PEER_HEADS * PEER_HALF, TM), f32), pltpu.VMEM((CAND_ROWS, TM), f32),
                        pltpu.VMEM((CAND_ROWS, TM), f32)],
        compiler_params=_cparams(("parallel",)),
        name="peer_route",
    )(x, g, shift, scale, wqt, k1, k2)


PEER_N2T = 32
PEER_SUB = 32
PEER_LC = LANE
PEER_NB = PEER_N // PEER_TE
PEER_UNITS = (T // PEER_TM) * PEER_NB
PEER_LAG = 2


def _zero_token(r):
    parts = [r[i:i + SUBLANE, j:j + PEER_LC] for i in range(0, r.shape[0], 2 * SUBLANE)
             for j in range(0, r.shape[1], 2 * LANE)]
    tok = jnp.minimum(jnp.abs(functools.reduce(jnp.add, parts)), 0.0)
    return jnp.concatenate([tok] * (PEER_SUB // SUBLANE), axis=0)


def _peer_dense_kernel(xmt_ref, u_ref, vt_ref, cnt1_ref, e1_ref, rank2_ref, e2_ref, x_ref, gm_ref, o_ref,
                       st, at, acc):
    f = pl.program_id(0)
    k = f - PEER_LAG

    @pl.when(f == 0)
    def _():
        for ref in (st, at):
            ref[...] = jnp.zeros_like(ref)

    @pl.when(jnp.logical_or(f == 0, k % PEER_NB == 0))
    def _():
        acc[...] = jnp.zeros_like(acc)

    n1_blocks = PEER_TE // PEER_NKEYS
    n_lc = PEER_TM // PEER_LC
    n_it = PEER_NKEYS // PEER_N2T
    e_rows = PEER_TE // n_it
    d_rows = D // n_it

    def phases(st_new, st_cur, at_cur, at_old):
        def tile(idx, _):
            rows_e = pl.ds(pl.multiple_of(idx * e_rows, e_rows), e_rows)
            rows_d = pl.ds(pl.multiple_of(idx * d_rows, d_rows), d_rows)
            r1 = jnp.dot(u_ref[rows_e, :].astype(bf16), xmt_ref[...], preferred_element_type=f32)
            st_new[rows_e, :] = r1
            r2 = jnp.dot(vt_ref[rows_d, :], at_old[...], preferred_element_type=f32)
            acc[rows_d, :] += r2
            order_after = {n_lc - 2: _zero_token(r1), n_lc - 1: _zero_token(r2)}
            n2_0 = pl.multiple_of(idx * PEER_N2T, PEER_N2T)
            for lc in range(n_lc):
                lanes = slice(lc * PEER_LC, (lc + 1) * PEER_LC)
                init = order_after.get(lc, jnp.zeros((PEER_SUB, PEER_LC), f32)).astype(bf16)
                for sub in range(0, PEER_N2T, PEER_SUB):
                    gates = [init for _ in range(n1_blocks)]
                    for h in range(PEER_HEADS):
                        r2 = rank2_ref[h, pl.ds(n2_0 + sub, PEER_SUB), lanes].astype(bf16)
                        e2 = e2_ref[h, pl.ds(n2_0 + sub, PEER_SUB), lanes].astype(bf16)
                        for a in range(n1_blocks):
                            cnt = jnp.broadcast_to(cnt1_ref[h, a:a + 1, lanes], (PEER_SUB, PEER_LC)).astype(bf16)
                            e1 = jnp.broadcast_to(e1_ref[h, a:a + 1, lanes], (PEER_SUB, PEER_LC)).astype(bf16)
                            gates[a] = gates[a] + jnp.where(r2 < cnt, e2 * e1, jnp.zeros((), bf16))
                    for a in range(n1_blocks):
                        r0 = a * PEER_NKEYS + n2_0 + sub
                        s = st_cur[pl.ds(r0, PEER_SUB), lanes].astype(bf16)
                        at_cur[pl.ds(r0, PEER_SUB), lanes] = gates[a] * _gelu(s)
            return 0

        lax.fori_loop(0, n_it, tile, 0)

    w = f % 2
    phases(st.at[w], st.at[1 - w], at.at[1 - w], at.at[w])

    @pl.when(jnp.logical_and(k >= 0, k % PEER_NB == PEER_NB - 1))
    def _():
        o_ref[...] = x_ref[...] + gm_ref[...] * acc[...].T


def _peer_tile_group(i):
    return jnp.where(i < NP // PEER_TM, 0, 1 + (i - NP // PEER_TM) // (SAMPLE_LEN // PEER_TM))


def _peer_dense(xmt, u_all, vt_all, layer, cnt1, e1, rank2, e2, x, gmod):
    n1b = PEER_TE // PEER_NKEYS

    def blk(f, lag):
        return jnp.clip(f - lag, 0, PEER_UNITS - 1) % PEER_NB

    def tok(f, lag):
        return jnp.clip(f - lag, 0, PEER_UNITS - 1) // PEER_NB

    row_spec = pl.BlockSpec((PEER_HEADS, n1b, PEER_TM), lambda f: (0, blk(f, 1), tok(f, 1)))
    full_spec = pl.BlockSpec((PEER_HEADS, PEER_NKEYS, PEER_TM), lambda f: (0, 0, tok(f, 1)))
    return pl.pallas_call(
        _peer_dense_kernel,
        grid=(PEER_UNITS + PEER_LAG,),
        in_specs=[pl.BlockSpec((D, PEER_TM), lambda f: (0, tok(f, 0))),
                  pl.BlockSpec((None, PEER_TE, D), lambda f: (layer, blk(f, 0), 0)),
                  pl.BlockSpec((None, D, PEER_TE), lambda f: (layer, 0, blk(f, PEER_LAG))),
                  row_spec, row_spec, full_spec, full_spec,
                  pl.BlockSpec((PEER_TM, D), lambda f: (tok(f, PEER_LAG), 0)),
                  pl.BlockSpec((None, 1, D), lambda f: (_peer_tile_group(tok(f, PEER_LAG)), 0, 0))],
        out_specs=pl.BlockSpec((PEER_TM, D), lambda f: (tok(f, PEER_LAG), 0)),
        out_shape=jax.ShapeDtypeStruct((T, D), f32),
        scratch_shapes=[pltpu.VMEM((2, PEER_TE, PEER_TM), f32), pltpu.VMEM((2, PEER_TE, PEER_TM), bf16),
                        pltpu.VMEM((D, PEER_TM), f32)],
        compiler_params=_cparams(("arbitrary",)),
        name="peer_dense",
    )(xmt, u_all, vt_all, cnt1, e1, rank2, e2, x, gmod)


def _final_norm_kernel(x_ref, g_ref, o_ref):
    x = x_ref[...]
    o_ref[...] = (x * lax.rsqrt(jnp.mean(x * x, axis=-1, keepdims=True) + EPS)) * g_ref[...]


def _final_norm(x, g, row0, n_rows):
    off = row0 // TM
    return pl.pallas_call(
        _final_norm_kernel,
        grid=(n_rows // TM,),
        in_specs=[pl.BlockSpec((TM, D), lambda i: (off + i, 0)), pl.BlockSpec((1, D), lambda i: (0, 0))],
        out_specs=pl.BlockSpec((TM, D), lambda i: (i, 0)),
        out_shape=jax.ShapeDtypeStruct((n_rows, D), f32),
        compiler_params=_cparams(("parallel",)),
        name="final_norm",
    )(x, g)


def _grid_pos_embed(rows, dim):
    t = jnp.arange(rows * GRID_W)
    r = (t // GRID_W).astype(f32)
    col = (t % GRID_W).astype(f32)
    nf = dim // 4
    freq = 1.0 / (10000.0 ** (jnp.arange(nf, dtype=f32) / nf))
    ar = r[:, None] * freq
    ac = col[:, None] * freq
    return jnp.concatenate([jnp.sin(ar), jnp.cos(ar), jnp.sin(ac), jnp.cos(ac)], axis=-1)


def _to_col_major(x, rows):
    b, n, d = x.shape
    return x.reshape(b, rows, GRID_W, d).swapaxes(1, 2).reshape(b, n, d)


def _from_col_major(x, rows):
    b, n, d = x.shape
    return x.reshape(b, GRID_W, rows, d).swapaxes(1, 2).reshape(b, n, d)


def kernel(x_prompt, x_sample, state_rglru, state_gla, c, c_ctx, norm1_g, norm2_g, ada_w, ada_b, peer_wq, peer_k1, peer_k2, peer_u, peer_v, rg_w_in, rg_conv_w, rg_conv_b, rg_wa, rg_ba, rg_wi, rg_bi, rg_lambda, rg_w_out, gla_w_in, gla_w_alpha, gla_b_alpha, gla_norm_g, gla_w_out, final_norm_g):
    rows = SAMPLE_LEN // GRID_W
    xs = x_sample + _grid_pos_embed(rows, D)[None]
    x = jnp.concatenate([x_prompt.reshape(NP, D), xs.reshape(NS, D)], axis=0)

    cond = jnp.concatenate([c_ctx[None], c, jnp.zeros((SUBLANE - N_GROUPS, D), f32)], axis=0)
    mods = _ada_all(cond.T, ada_w, ada_b).reshape(DEPTH, N_GROUPS, N_ADA, 1, D)

    vt_all = peer_v.transpose(0, 2, 1).astype(bf16)
    rg_new, gla_new = [], []
    for l in range(DEPTH):
        j = l // 2
        mod = [mods[l, :, i] for i in range(N_ADA)]
        if l % 2 == 0:
            proj = _nm_matmul(x, norm1_g[l][None], mod[0], mod[1], rg_w_in[j].astype(bf16), "rg_in")
            w4, b4 = _rg_gate_weights(rg_wa[j], rg_ba[j], rg_wi[j], rg_bi[j])
            h0 = jnp.concatenate([jnp.zeros((NP // RG_SEQ, 2, D), f32), state_rglru[:, j]], axis=0)
            y, fin = _rg_scan(proj, rg_conv_w[j], rg_conv_b[j][None], w4, b4, rg_lambda[j], h0)
            st = fin[:NP // RG_SEQ].reshape(NP // RG_SEQ, 2, RG_SEQ // PROMPT_LEN, D)
            rg_new.append(st.transpose(0, 2, 1, 3).reshape(N_PROMPT_SEQ, 2, D))
            x = _res_matmul(y, rg_w_out[j].astype(bf16), x, mod[2], "rg_out")
        else:
            w_in = jnp.concatenate([gla_w_in[j], jnp.zeros((D, GLA_IN_PAD - GLA_IN), f32)], axis=1).astype(bf16)
            proj = _nm_matmul(x, norm1_g[l][None], mod[0], mod[1], w_in, "gla_in")
            wal = jnp.zeros((2, LANE, GLA_QK), f32)
            wal = wal.at[0, 0:GLA_RANK].set(gla_w_alpha[j, 0]).at[1, GLA_RANK:2 * GLA_RANK].set(gla_w_alpha[j, 1])
            bal = gla_b_alpha[j][:, None, :]
            ng = gla_norm_g[j][None]
            o_p, st = _gla_attn(proj, wal, bal, ng, None, N_PROMPT_SEQ, PROMPT_LEN, 0)
            gla_new.append(st)
            proj_s = _to_col_major(proj[NP:].reshape(N_SAMPLE_SEQ, SAMPLE_LEN, GLA_IN_PAD), rows).reshape(NS, GLA_IN_PAD)
            o_s, _ = _gla_attn(proj_s, wal, bal, ng, state_gla[:, j], N_SAMPLE_SEQ, SAMPLE_LEN, 0)
            o_s = _from_col_major(o_s.reshape(N_SAMPLE_SEQ, SAMPLE_LEN, GLA_V), rows).reshape(NS, GLA_V)
            y = jnp.concatenate([o_p, o_s], axis=0)
            x = _res_matmul(y, gla_w_out[j].astype(bf16), x, mod[2], "gla_out")
        xmt, cnt1, e1, rank2, e2 = _peer_route(x, norm2_g[l][None], mod[3], mod[4], peer_wq[l].T.astype(bf16),
                                             peer_k1[l].astype(bf16), peer_k2[l].astype(bf16))
        x = _peer_dense(xmt, peer_u, vt_all, l, cnt1, e1, rank2, e2, x, mod[5])

    y_prompt = _final_norm(x, final_norm_g[None], 0, NP).reshape(N_PROMPT_SEQ, PROMPT_LEN, D)
    y_sample = _final_norm(x, final_norm_g[None], NP, NS).reshape(N_SAMPLE_SEQ, SAMPLE_LEN, D)
    return (y_prompt, y_sample, jnp.stack(rg_new, axis=1), jnp.stack(gla_new, axis=1))
```

```python
import functools
import math

import jax
import jax.numpy as jnp
from jax import lax
from jax.experimental import pallas as pl
from jax.experimental.pallas import tpu as pltpu

f32 = jnp.float32
bf16 = jnp.bfloat16

D = 1024
N_PROMPT_SEQ, PROMPT_LEN = 16, 256
N_SAMPLE_SEQ, SAMPLE_LEN = 2, 1024
NP = N_PROMPT_SEQ * PROMPT_LEN
NS = N_SAMPLE_SEQ * SAMPLE_LEN
T = NP + NS
DEPTH = 4
GRID_W = 64
N_ADA = 6
N_GROUPS = 1 + N_SAMPLE_SEQ
EPS = 1e-6
CONV_W, CONV_LEFT = 4, 2
RG_BLOCKS, RG_BW = 16, 64
RG_C = 8.0
GLA_HEADS, GLA_DK, GLA_DV = 4, 128, 256
GLA_QK, GLA_V = 512, 1024
GLA_RANK = 16
GLA_TAU = 16.0
GLA_CHUNK = 64
GLA_IN = 2 * GLA_QK + 2 * GLA_V + 2 * GLA_RANK
GLA_IN_PAD = 2 * GLA_QK + 2 * GLA_V + 128
PEER_HEADS, PEER_NKEYS, PEER_HALF, PEER_TOPK = 8, 128, 128, 16
PEER_N = PEER_NKEYS * PEER_NKEYS

LANE = 128
SUBLANE = 8
VMEM_LIMIT = 52 * 1024 * 1024

TM = 256
RG_SEQ = 1024
RG_TC = 512
PEER_TM = 512
PEER_TE = 1024
CAND = [(a, b) for a in range(PEER_TOPK) for b in range(PEER_TOPK) if (a + 1) * (b + 1) <= PEER_TOPK]
CAND_ROWS = -(-len(CAND) // SUBLANE) * SUBLANE


def _cparams(sem, flags=None):
    return pltpu.CompilerParams(dimension_semantics=sem, vmem_limit_bytes=VMEM_LIMIT, flags=flags)


def _tile_group(i):
    return jnp.where(i < NP // TM, 0, 1 + (i - NP // TM) // (SAMPLE_LEN // TM))


def _gelu(x):
    return 0.5 * x * (1.0 + jnp.tanh(0.7978845608028654 * (x + 0.044715 * (x * x * x))))


def _sigmoid(x):
    return 1.0 / (1.0 + jnp.exp(-x))


def _softplus(x):
    return jnp.maximum(x, 0.0) + jnp.log1p(jnp.exp(-jnp.abs(x)))


def _normmod(x, g, shift, scale):
    ms = jnp.mean(x * x, axis=-1, keepdims=True)
    y = (x * lax.rsqrt(ms + EPS)) * g
    return y * (1.0 + scale) + shift


def _ada_kernel(ct_ref, w_ref, b_ref, o_ref):
    ct = ct_ref[...]
    s = ct * _sigmoid(ct)
    w = w_ref[...]
    rows = [jnp.sum(w * s[:, r:r + 1], axis=0, keepdims=True) for r in range(N_GROUPS)]
    o_ref[...] = jnp.concatenate(rows, axis=0) + b_ref[...]


def _ada_all(cond_t, ada_w, ada_b):
    tn = 1536
    nd = N_ADA * D
    return pl.pallas_call(
        _ada_kernel,
        grid=(DEPTH, nd // tn),
        in_specs=[pl.BlockSpec((D, SUBLANE), lambda l, j: (0, 0)),
                  pl.BlockSpec((None, D, tn), lambda l, j: (l, 0, j)),
                  pl.BlockSpec((None, 1, tn), lambda l, j: (l, 0, j))],
        out_specs=pl.BlockSpec((None, N_GROUPS, tn), lambda l, j: (l, 0, j)),
        out_shape=jax.ShapeDtypeStruct((DEPTH, N_GROUPS, nd), f32),
        compiler_params=_cparams(("parallel", "parallel")),
        name="ada_mod",
    )(cond_t, ada_w, ada_b.reshape(DEPTH, 1, nd))


def _nm_matmul_kernel(x_ref, g_ref, sh_ref, sc_ref, w_ref, o_ref):
    xm = _normmod(x_ref[...], g_ref[...], sh_ref[...], sc_ref[...])
    o_ref[...] = jnp.dot(xm.astype(bf16), w_ref[...], preferred_element_type=f32)


def _nm_matmul(x, g, shift, scale, w, name):
    n = w.shape[1]
    return pl.pallas_call(
        _nm_matmul_kernel,
        grid=(T // TM,),
        in_specs=[pl.BlockSpec((TM, D), lambda i: (i, 0)),
                  pl.BlockSpec((1, D), lambda i: (0, 0)),
                  pl.BlockSpec((None, 1, D), lambda i: (_tile_group(i), 0, 0)),
                  pl.BlockSpec((None, 1, D), lambda i: (_tile_group(i), 0, 0)),
                  pl.BlockSpec((D, n), lambda i: (0, 0))],
        out_specs=pl.BlockSpec((TM, n), lambda i: (i, 0)),
        out_shape=jax.ShapeDtypeStruct((T, n), f32),
        compiler_params=_cparams(("parallel",)),
        name=name,
    )(x, g, shift, scale, w)


def _res_matmul_kernel(y_ref, w_ref, x_ref, gm_ref, o_ref):
    d = jnp.dot(y_ref[...], w_ref[...], preferred_element_type=f32)
    o_ref[...] = x_ref[...] + gm_ref[...] * d


def _res_matmul(y, w, x, gmod, name):
    k = y.shape[1]
    return pl.pallas_call(
        _res_matmul_kernel,
        grid=(T // TM,),
        in_specs=[pl.BlockSpec((TM, k), lambda i: (i, 0)),
                  pl.BlockSpec((k, D), lambda i: (0, 0)),
                  pl.BlockSpec((TM, D), lambda i: (i, 0)),
                  pl.BlockSpec((None, 1, D), lambda i: (_tile_group(i), 0, 0))],
        out_specs=pl.BlockSpec((TM, D), lambda i: (i, 0)),
        out_shape=jax.ShapeDtypeStruct((T, D), f32),
        compiler_params=_cparams(("parallel",)),
        name=name,
    )(y, w, x, gmod)


def _rg_kernel(gp_ref, xp_ref, cw_ref, cb_ref, w4_ref, b4_ref, lam_ref, h0_ref, y_ref, fin_ref,
               pad, a_f, b_f, a_b, b_b, h_f, h_b):
    L, tc = RG_SEQ, RG_TC
    is_prompt = pl.program_id(0) < NP // RG_SEQ
    far = jnp.where(is_prompt, 0, L)
    pad[0:SUBLANE, :] = jnp.zeros((SUBLANE, tc), f32)
    pad[SUBLANE + L:2 * SUBLANE + L, :] = jnp.zeros((SUBLANE, tc), f32)
    pad[SUBLANE:SUBLANE + L, :] = xp_ref[...]
    pos = lax.broadcasted_iota(jnp.int32, (L, tc), 0) & (PROMPT_LEN - 1)
    xr = jnp.zeros((L, tc), f32) + cb_ref[...]
    for j in range(CONV_W):
        off = j - CONV_LEFT
        xs = pad[pl.ds(SUBLANE + off, L), :]
        if off < 0:
            xs = jnp.where(pos + far >= -off, xs, 0.0)
        elif off > 0:
            xs = jnp.where(pos - far < PROMPT_LEN - off, xs, 0.0)
        xr = xr + xs * cw_ref[j:j + 1, :]
    pre = jnp.dot(xr.astype(bf16), w4_ref[...], preferred_element_type=f32) + b4_ref[...]
    sp = _softplus(-lam_ref[...])
    for d, (a_ref, b_ref, reset_pos) in enumerate(((a_f, b_f, 0), (a_b, b_b, PROMPT_LEN - 1))):
        r = _sigmoid(pre[:, (2 * d) * tc:(2 * d + 1) * tc])
        i = _sigmoid(pre[:, (2 * d + 1) * tc:(2 * d + 2) * tc])
        a = jnp.exp(-RG_C * r * sp[d:d + 1, :])
        b = jnp.sqrt(1.0 - a * a) * (i * xr)
        a_ref[...] = jnp.where(pos + far == reset_pos, 0.0, a)
        b_ref[...] = b

    def body(t, carry):
        hf, hb = carry
        base = pl.multiple_of(t * SUBLANE, SUBLANE)
        rbase = pl.multiple_of(L - SUBLANE - t * SUBLANE, SUBLANE)
        for k in range(SUBLANE):
            rf = base + k
            hf = a_f[pl.ds(rf, 1), :] * hf + b_f[pl.ds(rf, 1), :]
            h_f[pl.ds(rf, 1), :] = hf
            rb = rbase + (SUBLANE - 1 - k)
            hb = a_b[pl.ds(rb, 1), :] * hb + b_b[pl.ds(rb, 1), :]
            h_b[pl.ds(rb, 1), :] = hb
        return hf, hb

    h0 = h0_ref[...]
    lax.fori_loop(0, L // SUBLANE, body, (h0[0:1, :], h0[1:2, :]))
    y_ref[...] = (_gelu(gp_ref[...]) * (h_f[...] + h_b[...])).astype(bf16)
    for q in range(L // PROMPT_LEN):
        fin_ref[q:q + 1, :] = h_f[(q + 1) * PROMPT_LEN - 1:(q + 1) * PROMPT_LEN, :]
        fin_ref[4 + q:5 + q, :] = h_b[q * PROMPT_LEN:q * PROMPT_LEN + 1, :]


def _rg_scan(proj, conv_w, conv_b, w4, b4, lam, h0):
    ncb = D // RG_TC
    L = RG_SEQ
    return pl.pallas_call(
        _rg_kernel,
        grid=(T // L, ncb),
        in_specs=[pl.BlockSpec((L, RG_TC), lambda s, c: (s, c)),
                  pl.BlockSpec((L, RG_TC), lambda s, c: (s, ncb + c)),
                  pl.BlockSpec((CONV_W, RG_TC), lambda s, c: (0, c)),
                  pl.BlockSpec((1, RG_TC), lambda s, c: (0, c)),
                  pl.BlockSpec((None, RG_TC, 4 * RG_TC), lambda s, c: (c, 0, 0)),
                  pl.BlockSpec((None, 1, 4 * RG_TC), lambda s, c: (c, 0, 0)),
                  pl.BlockSpec((2, RG_TC), lambda s, c: (0, c)),
                  pl.BlockSpec((None, 2, RG_TC), lambda s, c: (s, 0, c))],
        out_specs=[pl.BlockSpec((L, RG_TC), lambda s, c: (s, c)),
                   pl.BlockSpec((None, SUBLANE, RG_TC), lambda s, c: (s, 0, c))],
        out_shape=[jax.ShapeDtypeStruct((T, D), bf16),
                   jax.ShapeDtypeStruct((T // L, SUBLANE, D), f32)],
        scratch_shapes=[pltpu.VMEM((L + 2 * SUBLANE, RG_TC), f32)] + [pltpu.VMEM((L, RG_TC), f32)] * 6,
        compiler_params=_cparams(("parallel", "parallel")),
        name="rg_scan",
    )(proj, proj, conv_w, conv_b, w4, b4, lam, h0)


def _rg_gate_weights(wa, ba, wi, bi):
    ncb = D // RG_TC
    per = RG_TC // RG_BW
    eye = jnp.eye(per, dtype=f32)

    def bd(w):
        w = w.reshape(ncb, per, RG_BW, RG_BW)
        return jnp.einsum('cpij,pq->cpiqj', w, eye).reshape(ncb, RG_TC, RG_TC)

    w4 = jnp.concatenate([bd(wa[0]), bd(wi[0]), bd(wa[1]), bd(wi[1])], axis=-1).astype(bf16)
    b4 = jnp.concatenate([v.reshape(ncb, 1, RG_TC) for v in (ba[0], bi[0], ba[1], bi[1])], axis=-1)
    return w4, b4


GLA_PREP_UNROLL = 4


def _gla_kernel(*refs, L, has_s0):
    if has_s0:
        (q_ref, k_ref, v_ref, g_ref, z_ref, wal_ref, bal_ref, ng_ref, s0_ref, o_ref, sf_ref,
         la_f, la_b, o_f, o_b, qin_f, qin_b, u_f, u_b, gb_f, gb_b, s_f, s_b) = refs
    else:
        (q_ref, k_ref, v_ref, g_ref, z_ref, wal_ref, bal_ref, ng_ref, o_ref, sf_ref,
         la_f, la_b, o_f, o_b, qin_f, qin_b, u_f, u_b, gb_f, gb_b, s_f, s_b) = refs
    C = GLA_CHUNK
    n = L // C
    z = z_ref[...]
    for d, la_ref in enumerate((la_f, la_b)):
        pre = jnp.dot(z, wal_ref[d], precision=lax.Precision.HIGHEST, preferred_element_type=f32) + bal_ref[d]
        la_ref[...] = -_softplus(-pre) * (1.0 / GLA_TAU)
    ri = lax.broadcasted_iota(jnp.int32, (C, C), 0)
    ci = lax.broadcasted_iota(jnp.int32, (C, C), 1)

    for d, s_ref in enumerate((s_f, s_b)):
        s_ref[...] = s0_ref[d] if has_s0 else jnp.zeros((GLA_DK, GLA_DV), f32)

    def prepare(d, c, la_ref, qin_ref, u_ref, gb_ref, o_dst):
        causal = (ri >= ci) if d == 0 else (ci >= ri)
        r0 = pl.multiple_of(c * C, C)
        la = la_ref[pl.ds(r0, C), :]
        bcum = jnp.dot(causal.astype(f32), la, precision=lax.Precision.HIGHEST, preferred_element_type=f32)
        edge = (C - SUBLANE) if d == 0 else 0
        tot8 = bcum[edge:edge + SUBLANE, :]
        tot_row = tot8[SUBLANE - 1:SUBLANE, :] if d == 0 else tot8[0:1, :]
        g_col = jnp.exp(tot8).T[:, (SUBLANE - 1 if d == 0 else 0):(SUBLANE if d == 0 else 1)]
        gb_ref[c] = jnp.broadcast_to(g_col, (GLA_DK, LANE))
        q = q_ref[pl.ds(r0, C), :] * (GLA_DK ** -0.5)
        k = k_ref[pl.ds(r0, C), :]
        v = v_ref[pl.ds(r0, C), :].astype(bf16)
        q_in = (q * jnp.exp(bcum)).astype(bf16)
        qin_ref[pl.ds(r0, C), :] = q_in
        k_in = (k * jnp.exp(-bcum)).astype(bf16)
        k_st = (k * jnp.exp(tot_row - bcum)).astype(bf16)
        att = lax.dot_general(q_in, k_in, (((1,), (1,)), ((), ())), preferred_element_type=f32)
        att = jnp.where(causal, att, 0.0).astype(bf16)
        o_dst[pl.ds(r0, C), :] = jnp.dot(att, v, preferred_element_type=f32)
        u_ref[c] = lax.dot_general(k_st, v, (((0,), (0,)), ((), ())), preferred_element_type=f32)

    def prep_trip(t, _):
        for j in range(GLA_PREP_UNROLL):
            c = t * GLA_PREP_UNROLL + j
            prepare(0, c, la_f, qin_f, u_f, gb_f, o_f)
            prepare(1, c, la_b, qin_b, u_b, gb_b, o_b)
        return 0

    lax.fori_loop(0, n // GLA_PREP_UNROLL, prep_trip, 0)

    def scan_trip(t, _):
        for c, qin_ref, u_ref, gb_ref, s_ref, o_dst in ((t, qin_f, u_f, gb_f, s_f, o_f),
                                                        (n - 1 - t, qin_b, u_b, gb_b, s_b, o_b)):
            r0 = pl.multiple_of(c * C, C)
            s = s_ref[...]
            o_dst[pl.ds(r0, C), :] += jnp.dot(qin_ref[pl.ds(r0, C), :], s.astype(bf16), preferred_element_type=f32)
            g = gb_ref[c]
            s_ref[...] = jnp.concatenate([g] * (GLA_DV // LANE), axis=1) * s + u_ref[c]
        return 0

    lax.fori_loop(0, n, scan_trip, 0)
    sf_ref[0] = s_f[...]
    sf_ref[1] = s_b[...]
    o = o_f[...] + o_b[...]
    o = o * lax.rsqrt(jnp.mean(o * o, axis=-1, keepdims=True) + EPS) * ng_ref[...]
    g = g_ref[...]
    o_ref[...] = (g * _sigmoid(g) * o).astype(bf16)


def _gla_attn(proj, wal, bal, ng, s0, n_seq, L, row0):
    has_s0 = s0 is not None
    H = GLA_HEADS
    off = row0 // L
    n = L // GLA_CHUNK
    in_specs = [pl.BlockSpec((L, GLA_DK), lambda b, h: (off + b, h)),
                pl.BlockSpec((L, GLA_DK), lambda b, h: (off + b, H + h)),
                pl.BlockSpec((L, GLA_DV), lambda b, h: (off + b, GLA_V // GLA_DV + h)),
                pl.BlockSpec((L, GLA_DV), lambda b, h: (off + b, 2 * GLA_V // GLA_DV + h)),
                pl.BlockSpec((L, LANE), lambda b, h: (off + b, (2 * GLA_QK + 2 * GLA_V) // LANE)),
                pl.BlockSpec((2, LANE, GLA_DK), lambda b, h: (0, 0, h)),
                pl.BlockSpec((2, 1, GLA_DK), lambda b, h: (0, 0, h)),
                pl.BlockSpec((1, GLA_DV), lambda b, h: (0, h))]
    args = [proj, proj, proj, proj, proj, wal, bal, ng]
    if has_s0:
        in_specs.append(pl.BlockSpec((None, 2, None, GLA_DK, GLA_DV), lambda b, h: (b, 0, h, 0, 0)))
        args.append(s0)
    per_dir = [pltpu.VMEM((L, GLA_DK), f32), pltpu.VMEM((L, GLA_DV), f32), pltpu.VMEM((L, GLA_DK), bf16),
               pltpu.VMEM((n, GLA_DK, GLA_DV), f32), pltpu.VMEM((n, GLA_DK, LANE), f32),
               pltpu.VMEM((GLA_DK, GLA_DV), f32)]
    return pl.pallas_call(
        functools.partial(_gla_kernel, L=L, has_s0=has_s0),
        grid=(n_seq, H),
        in_specs=in_specs,
        out_specs=[pl.BlockSpec((L, GLA_DV), lambda b, h: (b, h)),
                   pl.BlockSpec((None, 2, None, GLA_DK, GLA_DV), lambda b, h: (b, 0, h, 0, 0))],
        out_shape=[jax.ShapeDtypeStruct((n_seq * L, GLA_V), bf16),
                   jax.ShapeDtypeStruct((n_seq, 2, H, GLA_DK, GLA_DV), f32)],
        scratch_shapes=[sc for kind in per_dir for sc in (kind, kind)],
        compiler_params=_cparams(("parallel", "parallel")),
        name="gla_attn_s0" if has_s0 else "gla_attn",
    )(*args)


def _top_vals(s, n):
    out = []
    cur = s
    for _ in range(n):
        m = jnp.max(cur, axis=0, keepdims=True)
        out.append(m)
        cur = jnp.where(cur == m, -jnp.inf, cur)
    return out


NOT_ROUTED = 64.0


def _top_vals_ranked(s, n):
    out = []
    cur = s
    rank = jnp.full(s.shape, NOT_ROUTED, f32)
    for k in range(n):
        m = jnp.max(cur, axis=0, keepdims=True)
        out.append(m)
        hit = cur == m
        rank = jnp.where(hit, float(k), rank)
        cur = jnp.where(hit, -jnp.inf, cur)
    return out, rank


def _top_exact(s, n):
    rows = lax.broadcasted_iota(jnp.int32, s.shape, 0)
    vals = []
    cur = s
    rank = jnp.full(s.shape, NOT_ROUTED, f32)
    for k in range(n):
        m = jnp.max(cur, axis=0, keepdims=True)
        first = jnp.min(jnp.where(cur == m, rows, s.shape[0]), axis=0, keepdims=True)
        one = rows == first
        vals.append(m)
        rank = jnp.where(one, float(k), rank)
        cur = jnp.where(one, -jnp.inf, cur)
    return vals, rank


def _count_ge(s, thr):
    return jnp.sum(jnp.where(s >= thr, 1.0, 0.0), axis=0, keepdims=True)


def _peer_route_kernel(x_ref, g_ref, sh_ref, sc_ref, wqt_ref, k1_ref, k2_ref,
                       xmt_ref, cnt1_ref, e1_ref, rank2_ref, e2_ref, qt, cand_a, cand_b):
    xm = _normmod(x_ref[...], g_ref[...], sh_ref[...], sc_ref[...])
    xmt = xm.T.astype(bf16)
    xmt_ref[...] = xmt
    qt[...] = jnp.dot(wqt_ref[...], xmt, preferred_element_type=f32)
    for cand in (cand_a, cand_b):
        cand[...] = jnp.full((CAND_ROWS, TM), -jnp.inf, f32)

    def scores(h):
        r0 = pl.multiple_of(h * 2 * PEER_HALF, 2 * PEER_HALF)
        q1 = qt[pl.ds(r0, PEER_HALF), :].astype(bf16)
        q2 = qt[pl.ds(r0 + PEER_HALF, PEER_HALF), :].astype(bf16)
        s1 = jnp.dot(k1_ref[h], q1, preferred_element_type=f32)
        s2 = jnp.dot(k2_ref[h], q2, preferred_element_type=f32)
        return s1, s2

    def softmax_norm(m):
        z = jnp.zeros((1, TM), f32)
        for kk in range(PEER_TOPK):
            z = z + jnp.exp(m[kk] - m[0])
        return z

    def head(h, cand):
        s1, s2 = scores(h)
        v1 = _top_vals(s1, PEER_TOPK)
        v2, rank2 = _top_vals_ranked(s2, PEER_TOPK)
        for idx, (a, b) in enumerate(CAND):
            cand[idx:idx + 1, :] = v1[a] + v2[b]
        m = _top_vals(cand[...], PEER_TOPK)
        z = softmax_norm(m)
        cnt = [jnp.zeros((1, TM), f32) for _ in range(PEER_TOPK)]
        for idx, (a, b) in enumerate(CAND):
            cnt[a] = cnt[a] + jnp.where(cand[idx:idx + 1, :] >= m[PEER_TOPK - 1], 1.0, 0.0)
        cnt1 = jnp.zeros((PEER_NKEYS, TM), f32)
        for a in range(PEER_TOPK):
            cnt1 = jnp.where(s1 == v1[a], cnt[a], cnt1)
        cnt1_ref[h] = cnt1
        e1_ref[h] = jnp.exp(s1 - v1[0])
        rank2_ref[h] = rank2
        e2_ref[h] = jnp.exp(s2 - v2[0]) / z
        k = float(PEER_TOPK)
        tied = jnp.logical_or(jnp.logical_or(_count_ge(s1, v1[PEER_TOPK - 1]) != k,
                                             _count_ge(s2, v2[PEER_TOPK - 1]) != k),
                              _count_ge(cand[...], m[PEER_TOPK - 1]) != k)
        return jnp.max(jnp.where(tied, 1.0, 0.0)) > 0.0

    def head_exact(h, cand):
        s1, s2 = scores(h)
        v1, rank1 = _top_exact(s1, PEER_TOPK)
        v2, rank2 = _top_exact(s2, PEER_TOPK)
        for idx, (a, b) in enumerate(CAND):
            cand[idx:idx + 1, :] = v1[a] + v2[b]
        m, rank_c = _top_exact(cand[...], PEER_TOPK)
        z = softmax_norm(m)
        cnt = [jnp.zeros((1, TM), f32) for _ in range(PEER_TOPK)]
        for idx, (a, b) in enumerate(CAND):
            cnt[a] = cnt[a] + jnp.where(rank_c[idx:idx + 1, :] < float(PEER_TOPK), 1.0, 0.0)
        cnt1 = jnp.zeros((PEER_NKEYS, TM), f32)
        for a in range(PEER_TOPK):
            cnt1 = jnp.where(rank1 == float(a), cnt[a], cnt1)
        cnt1_ref[h] = cnt1
        e1_ref[h] = jnp.exp(s1 - v1[0])
        rank2_ref[h] = rank2
        e2_ref[h] = jnp.exp(s2 - v2[0]) / z

    def head_pair(hh, _):
        tied_a = head(2 * hh, cand_a)
        tied_b = head(2 * hh + 1, cand_b)

        @pl.when(tied_a)
        def _():
            head_exact(2 * hh, cand_a)

        @pl.when(tied_b)
        def _():
            head_exact(2 * hh + 1, cand_b)

        return 0

    lax.fori_loop(0, PEER_HEADS // 2, head_pair, 0)


def _peer_route(x, g, shift, scale, wqt, k1, k2):
    side = jax.ShapeDtypeStruct((PEER_HEADS, PEER_NKEYS, T), f32)
    side_spec = pl.BlockSpec((PEER_HEADS, PEER_NKEYS, TM), lambda i: (0, 0, i))
    return pl.pallas_call(
        _peer_route_kernel,
        grid=(T // TM,),
        in_specs=[pl.BlockSpec((TM, D), lambda i: (i, 0)),
                  pl.BlockSpec((1, D), lambda i: (0, 0)),
                  pl.BlockSpec((None, 1, D), lambda i: (_tile_group(i), 0, 0)),
                  pl.BlockSpec((None, 1, D), lambda i: (_tile_group(i), 0, 0)),
                  pl.BlockSpec((2 * PEER_HEADS * PEER_HALF, D), lambda i: (0, 0)),
                  pl.BlockSpec((PEER_HEADS, PEER_NKEYS, PEER_HALF), lambda i: (0, 0, 0)),
                  pl.BlockSpec((PEER_HEADS, PEER_NKEYS, PEER_HALF), lambda i: (0, 0, 0))],
        out_specs=[pl.BlockSpec((D, TM), lambda i: (0, i)), side_spec, side_spec, side_spec, side_spec],
        out_shape=[jax.ShapeDtypeStruct((D, T), bf16), side, side, side, side],
        scratch_shapes=[pltpu.VMEM((2 * PEER_HEADS * PEER_HALF, TM), f32), pltpu.VMEM((CAND_ROWS, TM), f32),
                        pltpu.VMEM((CAND_ROWS, TM), f32)],
        compiler_params=_cparams(("parallel",)),
        name="peer_route",
    )(x, g, shift, scale, wqt, k1, k2)


PEER_N2T = 32
PEER_SUB = 32
PEER_LC = LANE
PEER_NB = PEER_N // PEER_TE
PEER_UNITS = (T // PEER_TM) * PEER_NB
PEER_LAG = 2


def _zero_token(r):
    parts = [r[i:i + SUBLANE, j:j + PEER_LC] for i in range(0, r.shape[0], 2 * SUBLANE)
             for j in range(0, r.shape[1], 2 * LANE)]
    tok = jnp.minimum(jnp.abs(functools.reduce(jnp.add, parts)), 0.0)
    return jnp.concatenate([tok] * (PEER_SUB // SUBLANE), axis=0)


def _peer_dense_kernel(xmt_ref, u_ref, v_ref, cnt1_ref, e1_ref, rank2_ref, e2_ref, x_ref, gm_ref, o_ref,
                       st, at, acc):
    f = pl.program_id(0)
    k = f - PEER_LAG

    @pl.when(f == 0)
    def _():
        for ref in (st, at):
            ref[...] = jnp.zeros_like(ref)

    @pl.when(jnp.logical_or(f == 0, k % PEER_NB == 0))
    def _():
        acc[...] = jnp.zeros_like(acc)

    n1_blocks = PEER_TE // PEER_NKEYS
    n_lc = PEER_TM // PEER_LC
    n_it = PEER_NKEYS // PEER_N2T
    e_rows = PEER_TE // n_it
    d_rows = D // n_it

    def phases(st_new, st_cur, at_cur, at_old):
        def tile(idx, _):
            rows_e = pl.ds(pl.multiple_of(idx * e_rows, e_rows), e_rows)
            rows_d = pl.ds(pl.multiple_of(idx * d_rows, d_rows), d_rows)
            r1 = jnp.dot(u_ref[rows_e, :].astype(bf16), xmt_ref[...], preferred_element_type=f32)
            st_new[rows_e, :] = r1
            r2 = lax.dot_general(v_ref[:, rows_d].astype(bf16), at_old[...], (((0,), (0,)), ((), ())),
                                 preferred_element_type=f32)
            acc[rows_d, :] += r2
            order_after = {n_lc - 2: _zero_token(r1), n_lc - 1: _zero_token(r2)}
            n2_0 = pl.multiple_of(idx * PEER_N2T, PEER_N2T)
            for lc in range(n_lc):
                lanes = slice(lc * PEER_LC, (lc + 1) * PEER_LC)
                init = order_after.get(lc, jnp.zeros((PEER_SUB, PEER_LC), f32)).astype(bf16)
                for sub in range(0, PEER_N2T, PEER_SUB):
                    gates = [init for _ in range(n1_blocks)]
                    for h in range(PEER_HEADS):
                        r2 = rank2_ref[h, pl.ds(n2_0 + sub, PEER_SUB), lanes].astype(bf16)
                        e2 = e2_ref[h, pl.ds(n2_0 + sub, PEER_SUB), lanes].astype(bf16)
                        for a in range(n1_blocks):
                            cnt = jnp.broadcast_to(cnt1_ref[h, a:a + 1, lanes], (PEER_SUB, PEER_LC)).astype(bf16)
                            e1 = jnp.broadcast_to(e1_ref[h, a:a + 1, lanes], (PEER_SUB, PEER_LC)).astype(bf16)
                            gates[a] = gates[a] + jnp.where(r2 < cnt, e2 * e1, jnp.zeros((), bf16))
                    for a in range(n1_blocks):
                        r0 = a * PEER_NKEYS + n2_0 + sub
                        s = st_cur[pl.ds(r0, PEER_SUB), lanes].astype(bf16)
                        at_cur[pl.ds(r0, PEER_SUB), lanes] = gates[a] * _gelu(s)
            return 0

        lax.fori_loop(0, n_it, tile, 0)

    w = f % 2
    phases(st.at[w], st.at[1 - w], at.at[1 - w], at.at[w])

    @pl.when(jnp.logical_and(k >= 0, k % PEER_NB == PEER_NB - 1))
    def _():
        o_ref[...] = x_ref[...] + gm_ref[...] * acc[...].T


def _peer_tile_group(i):
    return jnp.where(i < NP // PEER_TM, 0, 1 + (i - NP // PEER_TM) // (SAMPLE_LEN // PEER_TM))


def _peer_dense(xmt, u_all, v_all, layer, cnt1, e1, rank2, e2, x, gmod):
    n1b = PEER_TE // PEER_NKEYS

    def blk(f, lag):
        return jnp.clip(f - lag, 0, PEER_UNITS - 1) % PEER_NB

    def tok(f, lag):
        return jnp.clip(f - lag, 0, PEER_UNITS - 1) // PEER_NB

    row_spec = pl.BlockSpec((PEER_HEADS, n1b, PEER_TM), lambda f: (0, blk(f, 1), tok(f, 1)))
    full_spec = pl.BlockSpec((PEER_HEADS, PEER_NKEYS, PEER_TM), lambda f: (0, 0, tok(f, 1)))
    return pl.pallas_call(
        _peer_dense_kernel,
        grid=(PEER_UNITS + PEER_LAG,),
        in_specs=[pl.BlockSpec((D, PEER_TM), lambda f: (0, tok(f, 0))),
                  pl.BlockSpec((None, PEER_TE, D), lambda f: (layer, blk(f, 0), 0)),
                  pl.BlockSpec((None, PEER_TE, D), lambda f: (layer, blk(f, PEER_LAG), 0)),
                  row_spec, row_spec, full_spec, full_spec,
                  pl.BlockSpec((PEER_TM, D), lambda f: (tok(f, PEER_LAG), 0)),
                  pl.BlockSpec((None, 1, D), lambda f: (_peer_tile_group(tok(f, PEER_LAG)), 0, 0))],
        out_specs=pl.BlockSpec((PEER_TM, D), lambda f: (tok(f, PEER_LAG), 0)),
        out_shape=jax.ShapeDtypeStruct((T, D), f32),
        scratch_shapes=[pltpu.VMEM((2, PEER_TE, PEER_TM), f32), pltpu.VMEM((2, PEER_TE, PEER_TM), bf16),
                        pltpu.VMEM((D, PEER_TM), f32)],
        compiler_params=_cparams(("arbitrary",)),
        name="peer_dense",
    )(xmt, u_all, v_all, cnt1, e1, rank2, e2, x, gmod)


def _final_norm_kernel(x_ref, g_ref, o_ref):
    x = x_ref[...]
    o_ref[...] = (x * lax.rsqrt(jnp.mean(x * x, axis=-1, keepdims=True) + EPS)) * g_ref[...]


def _final_norm(x, g, row0, n_rows):
    off = row0 // TM
    return pl.pallas_call(
        _final_norm_kernel,
        grid=(n_rows // TM,),
        in_specs=[pl.BlockSpec((TM, D), lambda i: (off + i, 0)), pl.BlockSpec((1, D), lambda i: (0, 0))],
        out_specs=pl.BlockSpec((TM, D), lambda i: (i, 0)),
        out_shape=jax.ShapeDtypeStruct((n_rows, D), f32),
        compiler_params=_cparams(("parallel",)),
        name="final_norm",
    )(x, g)


def _grid_pos_embed(rows, dim):
    t = jnp.arange(rows * GRID_W)
    r = (t // GRID_W).astype(f32)
    col = (t % GRID_W).astype(f32)
    nf = dim // 4
    freq = 1.0 / (10000.0 ** (jnp.arange(nf, dtype=f32) / nf))
    ar = r[:, None] * freq
    ac = col[:, None] * freq
    return jnp.concatenate([jnp.sin(ar), jnp.cos(ar), jnp.sin(ac), jnp.cos(ac)], axis=-1)


def _to_col_major(x, rows):
    b, n, d = x.shape
    return x.reshape(b, rows, GRID_W, d).swapaxes(1, 2).reshape(b, n, d)


def _from_col_major(x, rows):
    b, n, d = x.shape
    return x.reshape(b, GRID_W, rows, d).swapaxes(1, 2).reshape(b, n, d)


def kernel(x_prompt, x_sample, state_rglru, state_gla, c, c_ctx, norm1_g, norm2_g, ada_w, ada_b, peer_wq, peer_k1, peer_k2, peer_u, peer_v, rg_w_in, rg_conv_w, rg_conv_b, rg_wa, rg_ba, rg_wi, rg_bi, rg_lambda, rg_w_out, gla_w_in, gla_w_alpha, gla_b_alpha, gla_norm_g, gla_w_out, final_norm_g):
    rows = SAMPLE_LEN // GRID_W
    xs = x_sample + _grid_pos_embed(rows, D)[None]
    x = jnp.concatenate([x_prompt.reshape(NP, D), xs.reshape(NS, D)], axis=0)

    cond = jnp.concatenate([c_ctx[None], c, jnp.zeros((SUBLANE - N_GROUPS, D), f32)], axis=0)
    mods = _ada_all(cond.T, ada_w, ada_b).reshape(DEPTH, N_GROUPS, N_ADA, 1, D)

    rg_new, gla_new = [], []
    for l in range(DEPTH):
        j = l // 2
        mod = [mods[l, :, i] for i in range(N_ADA)]
        if l % 2 == 0:
            proj = _nm_matmul(x, norm1_g[l][None], mod[0], mod[1], rg_w_in[j].astype(bf16), "rg_in")
            w4, b4 = _rg_gate_weights(rg_wa[j], rg_ba[j], rg_wi[j], rg_bi[j])
            h0 = jnp.concatenate([jnp.zeros((NP // RG_SEQ, 2, D), f32), state_rglru[:, j]], axis=0)
            y, fin = _rg_scan(proj, rg_conv_w[j], rg_conv_b[j][None], w4, b4, rg_lambda[j], h0)
            st = fin[:NP // RG_SEQ].reshape(NP // RG_SEQ, 2, RG_SEQ // PROMPT_LEN, D)
            rg_new.append(st.transpose(0, 2, 1, 3).reshape(N_PROMPT_SEQ, 2, D))
            x = _res_matmul(y, rg_w_out[j].astype(bf16), x, mod[2], "rg_out")
        else:
            w_in = jnp.concatenate([gla_w_in[j], jnp.zeros((D, GLA_IN_PAD - GLA_IN), f32)], axis=1).astype(bf16)
            proj = _nm_matmul(x, norm1_g[l][None], mod[0], mod[1], w_in, "gla_in")
            wal = jnp.zeros((2, LANE, GLA_QK), f32)
            wal = wal.at[0, 0:GLA_RANK].set(gla_w_alpha[j, 0]).at[1, GLA_RANK:2 * GLA_RANK].set(gla_w_alpha[j, 1])
            bal = gla_b_alpha[j][:, None, :]
            ng = gla_norm_g[j][None]
            o_p, st = _gla_attn(proj, wal, bal, ng, None, N_PROMPT_SEQ, PROMPT_LEN, 0)
            gla_new.append(st)
            proj_s = _to_col_major(proj[NP:].reshape(N_SAMPLE_SEQ, SAMPLE_LEN, GLA_IN_PAD), rows).reshape(NS, GLA_IN_PAD)
            o_s, _ = _gla_attn(proj_s, wal, bal, ng, state_gla[:, j], N_SAMPLE_SEQ, SAMPLE_LEN, 0)
            o_s = _from_col_major(o_s.reshape(N_SAMPLE_SEQ, SAMPLE_LEN, GLA_V), rows).reshape(NS, GLA_V)
            y = jnp.concatenate([o_p, o_s], axis=0)
            x = _res_matmul(y, gla_w_out[j].astype(bf16), x, mod[2], "gla_out")
        xmt, cnt1, e1, rank2, e2 = _peer_route(x, norm2_g[l][None], mod[3], mod[4], peer_wq[l].T.astype(bf16),
                                             peer_k1[l].astype(bf16), peer_k2[l].astype(bf16))
        x = _peer_dense(xmt, peer_u, peer_v, l, cnt1, e1, rank2, e2, x, mod[5])

    y_prompt = _final_norm(x, final_norm_g[None], 0, NP).reshape(N_PROMPT_SEQ, PROMPT_LEN, D)
    y_sample = _final_norm(x, final_norm_g[None], NP, NS).reshape(N_SAMPLE_SEQ, SAMPLE_LEN, D)
    return (y_prompt, y_sample, jnp.stack(rg_new, axis=1), jnp.stack(gla_new, axis=1))
```

```python
import functools
import math

import jax
import jax.numpy as jnp
from jax import lax
from jax.experimental import pallas as pl
from jax.experimental.pallas import tpu as pltpu

f32 = jnp.float32
bf16 = jnp.bfloat16

D = 1024
N_PROMPT_SEQ, PROMPT_LEN = 16, 256
N_SAMPLE_SEQ, SAMPLE_LEN = 2, 1024
NP = N_PROMPT_SEQ * PROMPT_LEN
NS = N_SAMPLE_SEQ * SAMPLE_LEN
T = NP + NS
DEPTH = 4
GRID_W = 64
N_ADA = 6
N_GROUPS = 1 + N_SAMPLE_SEQ
EPS = 1e-6
CONV_W, CONV_LEFT = 4, 2
RG_BLOCKS, RG_BW = 16, 64
RG_C = 8.0
GLA_HEADS, GLA_DK, GLA_DV = 4, 128, 256
GLA_QK, GLA_V = 512, 1024
GLA_RANK = 16
GLA_TAU = 16.0
GLA_CHUNK = 64
GLA_IN = 2 * GLA_QK + 2 * GLA_V + 2 * GLA_RANK
GLA_IN_PAD = 2 * GLA_QK + 2 * GLA_V + 128
PEER_HEADS, PEER_NKEYS, PEER_HALF, PEER_TOPK = 8, 128, 128, 16
PEER_N = PEER_NKEYS * PEER_NKEYS

LANE = 128
SUBLANE = 8
VMEM_LIMIT = 52 * 1024 * 1024

TM = 256
RG_SEQ = 1024
RG_TC = 512
PEER_TM = 512
PEER_TE = 1024
CAND = [(a, b) for a in range(PEER_TOPK) for b in range(PEER_TOPK) if (a + 1) * (b + 1) <= PEER_TOPK]
CAND_ROWS = -(-len(CAND) // SUBLANE) * SUBLANE


def _cparams(sem, flags=None):
    return pltpu.CompilerParams(dimension_semantics=sem, vmem_limit_bytes=VMEM_LIMIT, flags=flags)


def _tile_group(i):
    return jnp.where(i < NP // TM, 0, 1 + (i - NP // TM) // (SAMPLE_LEN // TM))


def _gelu(x):
    return 0.5 * x * (1.0 + jnp.tanh(0.7978845608028654 * (x + 0.044715 * (x * x * x))))


def _sigmoid(x):
    return 1.0 / (1.0 + jnp.exp(-x))


def _softplus(x):
    return jnp.maximum(x, 0.0) + jnp.log1p(jnp.exp(-jnp.abs(x)))


def _normmod(x, g, shift, scale):
    ms = jnp.mean(x * x, axis=-1, keepdims=True)
    y = (x * lax.rsqrt(ms + EPS)) * g
    return y * (1.0 + scale) + shift


def _ada_kernel(ct_ref, w_ref, b_ref, o_ref):
    ct = ct_ref[...]
    s = ct * _sigmoid(ct)
    w = w_ref[...]
    rows = [jnp.sum(w * s[:, r:r + 1], axis=0, keepdims=True) for r in range(N_GROUPS)]
    o_ref[...] = jnp.concatenate(rows, axis=0) + b_ref[...]


def _ada_all(cond_t, ada_w, ada_b):
    tn = 1536
    nd = N_ADA * D
    return pl.pallas_call(
        _ada_kernel,
        grid=(DEPTH, nd // tn),
        in_specs=[pl.BlockSpec((D, SUBLANE), lambda l, j: (0, 0)),
                  pl.BlockSpec((None, D, tn), lambda l, j: (l, 0, j)),
                  pl.BlockSpec((None, 1, tn), lambda l, j: (l, 0, j))],
        out_specs=pl.BlockSpec((None, N_GROUPS, tn), lambda l, j: (l, 0, j)),
        out_shape=jax.ShapeDtypeStruct((DEPTH, N_GROUPS, nd), f32),
        compiler_params=_cparams(("parallel", "parallel")),
        name="ada_mod",
    )(cond_t, ada_w, ada_b.reshape(DEPTH, 1, nd))


def _nm_matmul_kernel(x_ref, g_ref, sh_ref, sc_ref, w_ref, o_ref):
    xm = _normmod(x_ref[...], g_ref[...], sh_ref[...], sc_ref[...])
    o_ref[...] = jnp.dot(xm.astype(bf16), w_ref[...], preferred_element_type=f32)


def _nm_matmul(x, g, shift, scale, w, name):
    n = w.shape[1]
    return pl.pallas_call(
        _nm_matmul_kernel,
        grid=(T // TM,),
        in_specs=[pl.BlockSpec((TM, D), lambda i: (i, 0)),
                  pl.BlockSpec((1, D), lambda i: (0, 0)),
                  pl.BlockSpec((None, 1, D), lambda i: (_tile_group(i), 0, 0)),
                  pl.BlockSpec((None, 1, D), lambda i: (_tile_group(i), 0, 0)),
                  pl.BlockSpec((D, n), lambda i: (0, 0))],
        out_specs=pl.BlockSpec((TM, n), lambda i: (i, 0)),
        out_shape=jax.ShapeDtypeStruct((T, n), f32),
        compiler_params=_cparams(("parallel",)),
        name=name,
    )(x, g, shift, scale, w)


def _res_matmul_kernel(y_ref, w_ref, x_ref, gm_ref, o_ref):
    d = jnp.dot(y_ref[...], w_ref[...], preferred_element_type=f32)
    o_ref[...] = x_ref[...] + gm_ref[...] * d


def _res_matmul(y, w, x, gmod, name):
    k = y.shape[1]
    return pl.pallas_call(
        _res_matmul_kernel,
        grid=(T // TM,),
        in_specs=[pl.BlockSpec((TM, k), lambda i: (i, 0)),
                  pl.BlockSpec((k, D), lambda i: (0, 0)),
                  pl.BlockSpec((TM, D), lambda i: (i, 0)),
                  pl.BlockSpec((None, 1, D), lambda i: (_tile_group(i), 0, 0))],
        out_specs=pl.BlockSpec((TM, D), lambda i: (i, 0)),
        out_shape=jax.ShapeDtypeStruct((T, D), f32),
        compiler_params=_cparams(("parallel",)),
        name=name,
    )(y, w, x, gmod)


def _rg_kernel(gp_ref, xp_ref, cw_ref, cb_ref, w4_ref, b4_ref, lam_ref, h0_ref, y_ref, fin_ref,
               pad, a_f, b_f, a_b, b_b, h_f, h_b):
    L, tc = RG_SEQ, RG_TC
    is_prompt = pl.program_id(0) < NP // RG_SEQ
    far = jnp.where(is_prompt, 0, L)
    pad[0:SUBLANE, :] = jnp.zeros((SUBLANE, tc), f32)
    pad[SUBLANE + L:2 * SUBLANE + L, :] = jnp.zeros((SUBLANE, tc), f32)
    pad[SUBLANE:SUBLANE + L, :] = xp_ref[...]
    pos = lax.broadcasted_iota(jnp.int32, (L, tc), 0) & (PROMPT_LEN - 1)
    xr = jnp.zeros((L, tc), f32) + cb_ref[...]
    for j in range(CONV_W):
        off = j - CONV_LEFT
        xs = pad[pl.ds(SUBLANE + off, L), :]
        if off < 0:
            xs = jnp.where(pos + far >= -off, xs, 0.0)
        elif off > 0:
            xs = jnp.where(pos - far < PROMPT_LEN - off, xs, 0.0)
        xr = xr + xs * cw_ref[j:j + 1, :]
    pre = jnp.dot(xr.astype(bf16), w4_ref[...], preferred_element_type=f32) + b4_ref[...]
    sp = _softplus(-lam_ref[...])
    for d, (a_ref, b_ref, reset_pos) in enumerate(((a_f, b_f, 0), (a_b, b_b, PROMPT_LEN - 1))):
        r = _sigmoid(pre[:, (2 * d) * tc:(2 * d + 1) * tc])
        i = _sigmoid(pre[:, (2 * d + 1) * tc:(2 * d + 2) * tc])
        a = jnp.exp(-RG_C * r * sp[d:d + 1, :])
        b = jnp.sqrt(1.0 - a * a) * (i * xr)
        a_ref[...] = jnp.where(pos + far == reset_pos, 0.0, a)
        b_ref[...] = b

    def body(t, carry):
        hf, hb = carry
        base = pl.multiple_of(t * SUBLANE, SUBLANE)
        rbase = pl.multiple_of(L - SUBLANE - t * SUBLANE, SUBLANE)
        for k in range(SUBLANE):
            rf = base + k
            hf = a_f[pl.ds(rf, 1), :] * hf + b_f[pl.ds(rf, 1), :]
            h_f[pl.ds(rf, 1), :] = hf
            rb = rbase + (SUBLANE - 1 - k)
            hb = a_b[pl.ds(rb, 1), :] * hb + b_b[pl.ds(rb, 1), :]
            h_b[pl.ds(rb, 1), :] = hb
        return hf, hb

    h0 = h0_ref[...]
    lax.fori_loop(0, L // SUBLANE, body, (h0[0:1, :], h0[1:2, :]))
    y_ref[...] = (_gelu(gp_ref[...]) * (h_f[...] + h_b[...])).astype(bf16)
    for q in range(L // PROMPT_LEN):
        fin_ref[q:q + 1, :] = h_f[(q + 1) * PROMPT_LEN - 1:(q + 1) * PROMPT_LEN, :]
        fin_ref[4 + q:5 + q, :] = h_b[q * PROMPT_LEN:q * PROMPT_LEN + 1, :]


def _rg_scan(proj, conv_w, conv_b, w4, b4, lam, h0):
    ncb = D // RG_TC
    L = RG_SEQ
    return pl.pallas_call(
        _rg_kernel,
        grid=(T // L, ncb),
        in_specs=[pl.BlockSpec((L, RG_TC), lambda s, c: (s, c)),
                  pl.BlockSpec((L, RG_TC), lambda s, c: (s, ncb + c)),
                  pl.BlockSpec((CONV_W, RG_TC), lambda s, c: (0, c)),
                  pl.BlockSpec((1, RG_TC), lambda s, c: (0, c)),
                  pl.BlockSpec((None, RG_TC, 4 * RG_TC), lambda s, c: (c, 0, 0)),
                  pl.BlockSpec((None, 1, 4 * RG_TC), lambda s, c: (c, 0, 0)),
                  pl.BlockSpec((2, RG_TC), lambda s, c: (0, c)),
                  pl.BlockSpec((None, 2, RG_TC), lambda s, c: (s, 0, c))],
        out_specs=[pl.BlockSpec((L, RG_TC), lambda s, c: (s, c)),
                   pl.BlockSpec((None, SUBLANE, RG_TC), lambda s, c: (s, 0, c))],
        out_shape=[jax.ShapeDtypeStruct((T, D), bf16),
                   jax.ShapeDtypeStruct((T // L, SUBLANE, D), f32)],
        scratch_shapes=[pltpu.VMEM((L + 2 * SUBLANE, RG_TC), f32)] + [pltpu.VMEM((L, RG_TC), f32)] * 6,
        compiler_params=_cparams(("parallel", "parallel")),
        name="rg_scan",
    )(proj, proj, conv_w, conv_b, w4, b4, lam, h0)


def _rg_gate_weights(wa, ba, wi, bi):
    ncb = D // RG_TC
    per = RG_TC // RG_BW
    eye = jnp.eye(per, dtype=f32)

    def bd(w):
        w = w.reshape(ncb, per, RG_BW, RG_BW)
        return jnp.einsum('cpij,pq->cpiqj', w, eye).reshape(ncb, RG_TC, RG_TC)

    w4 = jnp.concatenate([bd(wa[0]), bd(wi[0]), bd(wa[1]), bd(wi[1])], axis=-1).astype(bf16)
    b4 = jnp.concatenate([v.reshape(ncb, 1, RG_TC) for v in (ba[0], bi[0], ba[1], bi[1])], axis=-1)
    return w4, b4


GLA_PREP_UNROLL = 4


def _gla_kernel(*refs, L, has_s0):
    if has_s0:
        (q_ref, k_ref, v_ref, g_ref, z_ref, wal_ref, bal_ref, ng_ref, s0_ref, o_ref, sf_ref,
         la_f, la_b, o_f, o_b, qin_f, qin_b, u_f, u_b, gb_f, gb_b, s_f, s_b) = refs
    else:
        (q_ref, k_ref, v_ref, g_ref, z_ref, wal_ref, bal_ref, ng_ref, o_ref, sf_ref,
         la_f, la_b, o_f, o_b, qin_f, qin_b, u_f, u_b, gb_f, gb_b, s_f, s_b) = refs
    C = GLA_CHUNK
    n = L // C
    z = z_ref[...]
    for d, la_ref in enumerate((la_f, la_b)):
        pre = jnp.dot(z, wal_ref[d], precision=lax.Precision.HIGHEST, preferred_element_type=f32) + bal_ref[d]
        la_ref[...] = -_softplus(-pre) * (1.0 / GLA_TAU)
    ri = lax.broadcasted_iota(jnp.int32, (C, C), 0)
    ci = lax.broadcasted_iota(jnp.int32, (C, C), 1)

    for d, s_ref in enumerate((s_f, s_b)):
        s_ref[...] = s0_ref[d] if has_s0 else jnp.zeros((GLA_DK, GLA_DV), f32)

    def prepare(d, c, la_ref, qin_ref, u_ref, gb_ref, o_dst):
        causal = (ri >= ci) if d == 0 else (ci >= ri)
        r0 = pl.multiple_of(c * C, C)
        la = la_ref[pl.ds(r0, C), :]
        bcum = jnp.dot(causal.astype(f32), la, precision=lax.Precision.HIGHEST, preferred_element_type=f32)
        edge = (C - SUBLANE) if d == 0 else 0
        tot8 = bcum[edge:edge + SUBLANE, :]
        tot_row = tot8[SUBLANE - 1:SUBLANE, :] if d == 0 else tot8[0:1, :]
        g_col = jnp.exp(tot8).T[:, (SUBLANE - 1 if d == 0 else 0):(SUBLANE if d == 0 else 1)]
        gb_ref[c] = jnp.broadcast_to(g_col, (GLA_DK, LANE))
        q = q_ref[pl.ds(r0, C), :] * (GLA_DK ** -0.5)
        k = k_ref[pl.ds(r0, C), :]
        v = v_ref[pl.ds(r0, C), :].astype(bf16)
        q_in = (q * jnp.exp(bcum)).astype(bf16)
        qin_ref[pl.ds(r0, C), :] = q_in
        k_in = (k * jnp.exp(-bcum)).astype(bf16)
        k_st = (k * jnp.exp(tot_row - bcum)).astype(bf16)
        att = lax.dot_general(q_in, k_in, (((1,), (1,)), ((), ())), preferred_element_type=f32)
        att = jnp.where(causal, att, 0.0).astype(bf16)
        o_dst[pl.ds(r0, C), :] = jnp.dot(att, v, preferred_element_type=f32)
        u_ref[c] = lax.dot_general(k_st, v, (((0,), (0,)), ((), ())), preferred_element_type=f32)

    def prep_trip(t, _):
        for j in range(GLA_PREP_UNROLL):
            c = t * GLA_PREP_UNROLL + j
            prepare(0, c, la_f, qin_f, u_f, gb_f, o_f)
            prepare(1, c, la_b, qin_b, u_b, gb_b, o_b)
        return 0

    lax.fori_loop(0, n // GLA_PREP_UNROLL, prep_trip, 0)

    def scan_trip(t, _):
        for c, qin_ref, u_ref, gb_ref, s_ref, o_dst in ((t, qin_f, u_f, gb_f, s_f, o_f),
                                                        (n - 1 - t, qin_b, u_b, gb_b, s_b, o_b)):
            r0 = pl.multiple_of(c * C, C)
            s = s_ref[...]
            o_dst[pl.ds(r0, C), :] += jnp.dot(qin_ref[pl.ds(r0, C), :], s.astype(bf16), preferred_element_type=f32)
            g = gb_ref[c]
            s_ref[...] = jnp.concatenate([g] * (GLA_DV // LANE), axis=1) * s + u_ref[c]
        return 0

    lax.fori_loop(0, n, scan_trip, 0)
    sf_ref[0] = s_f[...]
    sf_ref[1] = s_b[...]
    o = o_f[...] + o_b[...]
    o = o * lax.rsqrt(jnp.mean(o * o, axis=-1, keepdims=True) + EPS) * ng_ref[...]
    g = g_ref[...]
    o_ref[...] = (g * _sigmoid(g) * o).astype(bf16)


def _gla_attn(proj, wal, bal, ng, s0, n_seq, L, row0):
    has_s0 = s0 is not None
    H = GLA_HEADS
    off = row0 // L
    n = L // GLA_CHUNK
    in_specs = [pl.BlockSpec((L, GLA_DK), lambda b, h: (off + b, h)),
                pl.BlockSpec((L, GLA_DK), lambda b, h: (off + b, H + h)),
                pl.BlockSpec((L, GLA_DV), lambda b, h: (off + b, GLA_V // GLA_DV + h)),
                pl.BlockSpec((L, GLA_DV), lambda b, h: (off + b, 2 * GLA_V // GLA_DV + h)),
                pl.BlockSpec((L, LANE), lambda b, h: (off + b, (2 * GLA_QK + 2 * GLA_V) // LANE)),
                pl.BlockSpec((2, LANE, GLA_DK), lambda b, h: (0, 0, h)),
                pl.BlockSpec((2, 1, GLA_DK), lambda b, h: (0, 0, h)),
                pl.BlockSpec((1, GLA_DV), lambda b, h: (0, h))]
    args = [proj, proj, proj, proj, proj, wal, bal, ng]
    if has_s0:
        in_specs.append(pl.BlockSpec((None, 2, None, GLA_DK, GLA_DV), lambda b, h: (b, 0, h, 0, 0)))
        args.append(s0)
    per_dir = [pltpu.VMEM((L, GLA_DK), f32), pltpu.VMEM((L, GLA_DV), f32), pltpu.VMEM((L, GLA_DK), bf16),
               pltpu.VMEM((n, GLA_DK, GLA_DV), f32), pltpu.VMEM((n, GLA_DK, LANE), f32),
               pltpu.VMEM((GLA_DK, GLA_DV), f32)]
    return pl.pallas_call(
        functools.partial(_gla_kernel, L=L, has_s0=has_s0),
        grid=(n_seq, H),
        in_specs=in_specs,
        out_specs=[pl.BlockSpec((L, GLA_DV), lambda b, h: (b, h)),
                   pl.BlockSpec((None, 2, None, GLA_DK, GLA_DV), lambda b, h: (b, 0, h, 0, 0))],
        out_shape=[jax.ShapeDtypeStruct((n_seq * L, GLA_V), bf16),
                   jax.ShapeDtypeStruct((n_seq, 2, H, GLA_DK, GLA_DV), f32)],
        scratch_shapes=[sc for kind in per_dir for sc in (kind, kind)],
        compiler_params=_cparams(("parallel", "parallel")),
        name="gla_attn_s0" if has_s0 else "gla_attn",
    )(*args)


def _top_vals(s, n):
    out = []
    cur = s
    for _ in range(n):
        m = jnp.max(cur, axis=0, keepdims=True)
        out.append(m)
        cur = jnp.where(cur == m, -jnp.inf, cur)
    return out


NOT_ROUTED = 64.0


def _top_vals_ranked(s, n):
    out = []
    cur = s
    rank = jnp.full(s.shape, NOT_ROUTED, f32)
    for k in range(n):
        m = jnp.max(cur, axis=0, keepdims=True)
        out.append(m)
        hit = cur == m
        rank = jnp.where(hit, float(k), rank)
        cur = jnp.where(hit, -jnp.inf, cur)
    return out, rank


def _top_exact(s, n):
    rows = lax.broadcasted_iota(jnp.int32, s.shape, 0)
    vals = []
    cur = s
    rank = jnp.full(s.shape, NOT_ROUTED, f32)
    for k in range(n):
        m = jnp.max(cur, axis=0, keepdims=True)
        first = jnp.min(jnp.where(cur == m, rows, s.shape[0]), axis=0, keepdims=True)
        one = rows == first
        vals.append(m)
        rank = jnp.where(one, float(k), rank)
        cur = jnp.where(one, -jnp.inf, cur)
    return vals, rank


def _count_ge(s, thr):
    return jnp.sum(jnp.where(s >= thr, 1.0, 0.0), axis=0, keepdims=True)


def _peer_route_kernel(x_ref, g_ref, sh_ref, sc_ref, wqt_ref, k1_ref, k2_ref,
                       xmt_ref, cnt1_ref, e1_ref, rank2_ref, e2_ref, qt, cand_a, cand_b):
    xm = _normmod(x_ref[...], g_ref[...], sh_ref[...], sc_ref[...])
    xmt = xm.T.astype(bf16)
    xmt_ref[...] = xmt
    qt[...] = jnp.dot(wqt_ref[...], xmt, preferred_element_type=f32)
    for cand in (cand_a, cand_b):
        cand[...] = jnp.full((CAND_ROWS, TM), -jnp.inf, f32)

    def scores(h):
        r0 = pl.multiple_of(h * 2 * PEER_HALF, 2 * PEER_HALF)
        q1 = qt[pl.ds(r0, PEER_HALF), :].astype(bf16)
        q2 = qt[pl.ds(r0 + PEER_HALF, PEER_HALF), :].astype(bf16)
        s1 = jnp.dot(k1_ref[h], q1, preferred_element_type=f32)
        s2 = jnp.dot(k2_ref[h], q2, preferred_element_type=f32)
        return s1, s2

    def softmax_norm(m):
        z = jnp.zeros((1, TM), f32)
        for kk in range(PEER_TOPK):
            z = z + jnp.exp(m[kk] - m[0])
        return z

    def head(h, cand):
        s1, s2 = scores(h)
        v1 = _top_vals(s1, PEER_TOPK)
        v2, rank2 = _top_vals_ranked(s2, PEER_TOPK)
        for idx, (a, b) in enumerate(CAND):
            cand[idx:idx + 1, :] = v1[a] + v2[b]
        m = _top_vals(cand[...], PEER_TOPK)
        z = softmax_norm(m)
        cnt = [jnp.zeros((1, TM), f32) for _ in range(PEER_TOPK)]
        for idx, (a, b) in enumerate(CAND):
            cnt[a] = cnt[a] + jnp.where(cand[idx:idx + 1, :] >= m[PEER_TOPK - 1], 1.0, 0.0)
        cnt1 = jnp.zeros((PEER_NKEYS, TM), f32)
        for a in range(PEER_TOPK):
            cnt1 = jnp.where(s1 == v1[a], cnt[a], cnt1)
        cnt1_ref[h] = cnt1
        e1_ref[h] = jnp.exp(s1 - v1[0])
        rank2_ref[h] = rank2
        e2_ref[h] = jnp.exp(s2 - v2[0]) / z
        k = float(PEER_TOPK)
        tied = jnp.logical_or(jnp.logical_or(_count_ge(s1, v1[PEER_TOPK - 1]) != k,
                                             _count_ge(s2, v2[PEER_TOPK - 1]) != k),
                              _count_ge(cand[...], m[PEER_TOPK - 1]) != k)
        return jnp.max(jnp.where(tied, 1.0, 0.0)) > 0.0

    def head_exact(h, cand):
        s1, s2 = scores(h)
        v1, rank1 = _top_exact(s1, PEER_TOPK)
        v2, rank2 = _top_exact(s2, PEER_TOPK)
        for idx, (a, b) in enumerate(CAND):
            cand[idx:idx + 1, :] = v1[a] + v2[b]
        m, rank_c = _top_exact(cand[...], PEER_TOPK)
        z = softmax_norm(m)
        cnt = [jnp.zeros((1, TM), f32) for _ in range(PEER_TOPK)]
        for idx, (a, b) in enumerate(CAND):
            cnt[a] = cnt[a] + jnp.where(rank_c[idx:idx + 1, :] < float(PEER_TOPK), 1.0, 0.0)
        cnt1 = jnp.zeros((PEER_NKEYS, TM), f32)
        for a in range(PEER_TOPK):
            cnt1 = jnp.where(rank1 == float(a), cnt[a], cnt1)
        cnt1_ref[h] = cnt1
        e1_ref[h] = jnp.exp(s1 - v1[0])
        rank2_ref[h] = rank2
        e2_ref[h] = jnp.exp(s2 - v2[0]) / z

    def head_pair(hh, _):
        tied_a = head(2 * hh, cand_a)
        tied_b = head(2 * hh + 1, cand_b)

        @pl.when(tied_a)
        def _():
            head_exact(2 * hh, cand_a)

        @pl.when(tied_b)
        def _():
            head_exact(2 * hh + 1, cand_b)

        return 0

    lax.fori_loop(0, PEER_HEADS // 2, head_pair, 0)


def _peer_route(x, g, shift, scale, wqt, k1, k2):
    side = jax.ShapeDtypeStruct((PEER_HEADS, PEER_NKEYS, T), f32)
    side_spec = pl.BlockSpec((PEER_HEADS, PEER_NKEYS, TM), lambda i: (0, 0, i))
    return pl.pallas_call(
        _peer_route_kernel,
        grid=(T // TM,),
        in_specs=[pl.BlockSpec((TM, D), lambda i: (i, 0)),
                  pl.BlockSpec((1, D), lambda i: (0, 0)),
                  pl.BlockSpec((None, 1, D), lambda i: (_tile_group(i), 0, 0)),
                  pl.BlockSpec((None, 1, D), lambda i: (_tile_group(i), 0, 0)),
                  pl.BlockSpec((2 * PEER_HEADS * PEER_HALF, D), lambda i: (0, 0)),
                  pl.BlockSpec((PEER_HEADS, PEER_NKEYS, PEER_HALF), lambda i: (0, 0, 0)),
                  pl.BlockSpec((PEER_HEADS, PEER_NKEYS, PEER_HALF), lambda i: (0, 0, 0))],
        out_specs=[pl.BlockSpec((D, TM), lambda i: (0, i)), side_spec, side_spec, side_spec, side_spec],
        out_shape=[jax.ShapeDtypeStruct((D, T), bf16), side, side, side, side],
        scratch_shapes=[pltpu.VMEM((2 * PEER_HEADS * PEER_HALF, TM), f32), pltpu.VMEM((CAND_ROWS, TM), f32),
                        pltpu.VMEM((CAND_ROWS, TM), f32)],
        compiler_params=_cparams(("parallel",)),
        name="peer_route",
    )(x, g, shift, scale, wqt, k1, k2)


PEER_N2T = 64
PEER_SUB = 32
PEER_LC = LANE
PEER_NB = PEER_N // PEER_TE
PEER_UNITS = (T // PEER_TM) * PEER_NB
PEER_LAG = 2


def _zero_token(r):
    parts = [r[i:i + SUBLANE, j:j + PEER_LC] for i in range(0, r.shape[0], 2 * SUBLANE)
             for j in range(0, r.shape[1], 2 * LANE)]
    tok = jnp.minimum(jnp.abs(functools.reduce(jnp.add, parts)), 0.0)
    return jnp.concatenate([tok] * (PEER_SUB // SUBLANE), axis=0)


def _peer_dense_kernel(xmt_ref, u_ref, v_ref, cnt1_ref, e1_ref, rank2_ref, e2_ref, x_ref, gm_ref, o_ref,
                       st, at, acc):
    f = pl.program_id(0)
    k = f - PEER_LAG

    @pl.when(f == 0)
    def _():
        for ref in (st, at):
            ref[...] = jnp.zeros_like(ref)

    @pl.when(jnp.logical_or(f == 0, k % PEER_NB == 0))
    def _():
        acc[...] = jnp.zeros_like(acc)

    n1_blocks = PEER_TE // PEER_NKEYS
    n_lc = PEER_TM // PEER_LC
    n_it = PEER_NKEYS // PEER_N2T
    e_rows = PEER_TE // n_it
    d_rows = D // n_it

    def phases(st_new, st_cur, at_cur, at_old):
        def tile(idx, _):
            rows_e = pl.ds(pl.multiple_of(idx * e_rows, e_rows), e_rows)
            rows_d = pl.ds(pl.multiple_of(idx * d_rows, d_rows), d_rows)
            r1 = jnp.dot(u_ref[rows_e, :].astype(bf16), xmt_ref[...], preferred_element_type=f32)
            st_new[rows_e, :] = r1
            r2 = lax.dot_general(v_ref[:, rows_d].astype(bf16), at_old[...], (((0,), (0,)), ((), ())),
                                 preferred_element_type=f32)
            acc[rows_d, :] += r2
            order_after = {n_lc - 2: _zero_token(r1), n_lc - 1: _zero_token(r2)}
            n2_0 = pl.multiple_of(idx * PEER_N2T, PEER_N2T)
            for lc in range(n_lc):
                lanes = slice(lc * PEER_LC, (lc + 1) * PEER_LC)
                init = order_after.get(lc, jnp.zeros((PEER_SUB, PEER_LC), f32)).astype(bf16)
                for sub in range(0, PEER_N2T, PEER_SUB):
                    gates = [init for _ in range(n1_blocks)]
                    for h in range(PEER_HEADS):
                        r2 = rank2_ref[h, pl.ds(n2_0 + sub, PEER_SUB), lanes].astype(bf16)
                        e2 = e2_ref[h, pl.ds(n2_0 + sub, PEER_SUB), lanes].astype(bf16)
                        for a in range(n1_blocks):
                            cnt = jnp.broadcast_to(cnt1_ref[h, a:a + 1, lanes], (PEER_SUB, PEER_LC)).astype(bf16)
                            e1 = jnp.broadcast_to(e1_ref[h, a:a + 1, lanes], (PEER_SUB, PEER_LC)).astype(bf16)
                            gates[a] = gates[a] + jnp.where(r2 < cnt, e2 * e1, jnp.zeros((), bf16))
                    for a in range(n1_blocks):
                        r0 = a * PEER_NKEYS + n2_0 + sub
                        s = st_cur[pl.ds(r0, PEER_SUB), lanes].astype(bf16)
                        at_cur[pl.ds(r0, PEER_SUB), lanes] = gates[a] * _gelu(s)
            return 0

        lax.fori_loop(0, n_it, tile, 0)

    w = f % 2
    phases(st.at[w], st.at[1 - w], at.at[1 - w], at.at[w])

    @pl.when(jnp.logical_and(k >= 0, k % PEER_NB == PEER_NB - 1))
    def _():
        o_ref[...] = x_ref[...] + gm_ref[...] * acc[...].T


def _peer_tile_group(i):
    return jnp.where(i < NP // PEER_TM, 0, 1 + (i - NP // PEER_TM) // (SAMPLE_LEN // PEER_TM))


def _peer_dense(xmt, u_all, v_all, layer, cnt1, e1, rank2, e2, x, gmod):
    n1b = PEER_TE // PEER_NKEYS

    def blk(f, lag):
        return jnp.clip(f - lag, 0, PEER_UNITS - 1) % PEER_NB

    def tok(f, lag):
        return jnp.clip(f - lag, 0, PEER_UNITS - 1) // PEER_NB

    row_spec = pl.BlockSpec((PEER_HEADS, n1b, PEER_TM), lambda f: (0, blk(f, 1), tok(f, 1)))
    full_spec = pl.BlockSpec((PEER_HEADS, PEER_NKEYS, PEER_TM), lambda f: (0, 0, tok(f, 1)))
    return pl.pallas_call(
        _peer_dense_kernel,
        grid=(PEER_UNITS + PEER_LAG,),
        in_specs=[pl.BlockSpec((D, PEER_TM), lambda f: (0, tok(f, 0))),
                  pl.BlockSpec((None, PEER_TE, D), lambda f: (layer, blk(f, 0), 0)),
                  pl.BlockSpec((None, PEER_TE, D), lambda f: (layer, blk(f, PEER_LAG), 0)),
                  row_spec, row_spec, full_spec, full_spec,
                  pl.BlockSpec((PEER_TM, D), lambda f: (tok(f, PEER_LAG), 0)),
                  pl.BlockSpec((None, 1, D), lambda f: (_peer_tile_group(tok(f, PEER_LAG)), 0, 0))],
        out_specs=pl.BlockSpec((PEER_TM, D), lambda f: (tok(f, PEER_LAG), 0)),
        out_shape=jax.ShapeDtypeStruct((T, D), f32),
        scratch_shapes=[pltpu.VMEM((2, PEER_TE, PEER_TM), f32), pltpu.VMEM((2, PEER_TE, PEER_TM), bf16),
                        pltpu.VMEM((D, PEER_TM), f32)],
        compiler_params=_cparams(("arbitrary",)),
        name="peer_dense",
    )(xmt, u_all, v_all, cnt1, e1, rank2, e2, x, gmod)


def _final_norm_kernel(x_ref, g_ref, o_ref):
    x = x_ref[...]
    o_ref[...] = (x * lax.rsqrt(jnp.mean(x * x, axis=-1, keepdims=True) + EPS)) * g_ref[...]


def _final_norm(x, g, row0, n_rows):
    off = row0 // TM
    return pl.pallas_call(
        _final_norm_kernel,
        grid=(n_rows // TM,),
        in_specs=[pl.BlockSpec((TM, D), lambda i: (off + i, 0)), pl.BlockSpec((1, D), lambda i: (0, 0))],
        out_specs=pl.BlockSpec((TM, D), lambda i: (i, 0)),
        out_shape=jax.ShapeDtypeStruct((n_rows, D), f32),
        compiler_params=_cparams(("parallel",)),
        name="final_norm",
    )(x, g)


def _grid_pos_embed(rows, dim):
    t = jnp.arange(rows * GRID_W)
    r = (t // GRID_W).astype(f32)
    col = (t % GRID_W).astype(f32)
    nf = dim // 4
    freq = 1.0 / (10000.0 ** (jnp.arange(nf, dtype=f32) / nf))
    ar = r[:, None] * freq
    ac = col[:, None] * freq
    return jnp.concatenate([jnp.sin(ar), jnp.cos(ar), jnp.sin(ac), jnp.cos(ac)], axis=-1)


def _to_col_major(x, rows):
    b, n, d = x.shape
    return x.reshape(b, rows, GRID_W, d).swapaxes(1, 2).reshape(b, n, d)


def _from_col_major(x, rows):
    b, n, d = x.shape
    return x.reshape(b, GRID_W, rows, d).swapaxes(1, 2).reshape(b, n, d)


def kernel(x_prompt, x_sample, state_rglru, state_gla, c, c_ctx, norm1_g, norm2_g, ada_w, ada_b, peer_wq, peer_k1, peer_k2, peer_u, peer_v, rg_w_in, rg_conv_w, rg_conv_b, rg_wa, rg_ba, rg_wi, rg_bi, rg_lambda, rg_w_out, gla_w_in, gla_w_alpha, gla_b_alpha, gla_norm_g, gla_w_out, final_norm_g):
    rows = SAMPLE_LEN // GRID_W
    xs = x_sample + _grid_pos_embed(rows, D)[None]
    x = jnp.concatenate([x_prompt.reshape(NP, D), xs.reshape(NS, D)], axis=0)

    cond = jnp.concatenate([c_ctx[None], c, jnp.zeros((SUBLANE - N_GROUPS, D), f32)], axis=0)
    mods = _ada_all(cond.T, ada_w, ada_b).reshape(DEPTH, N_GROUPS, N_ADA, 1, D)

    rg_new, gla_new = [], []
    for l in range(DEPTH):
        j = l // 2
        mod = [mods[l, :, i] for i in range(N_ADA)]
        if l % 2 == 0:
            proj = _nm_matmul(x, norm1_g[l][None], mod[0], mod[1], rg_w_in[j].astype(bf16), "rg_in")
            w4, b4 = _rg_gate_weights(rg_wa[j], rg_ba[j], rg_wi[j], rg_bi[j])
            h0 = jnp.concatenate([jnp.zeros((NP // RG_SEQ, 2, D), f32), state_rglru[:, j]], axis=0)
            y, fin = _rg_scan(proj, rg_conv_w[j], rg_conv_b[j][None], w4, b4, rg_lambda[j], h0)
            st = fin[:NP // RG_SEQ].reshape(NP // RG_SEQ, 2, RG_SEQ // PROMPT_LEN, D)
            rg_new.append(st.transpose(0, 2, 1, 3).reshape(N_PROMPT_SEQ, 2, D))
            x = _res_matmul(y, rg_w_out[j].astype(bf16), x, mod[2], "rg_out")
        else:
            w_in = jnp.concatenate([gla_w_in[j], jnp.zeros((D, GLA_IN_PAD - GLA_IN), f32)], axis=1).astype(bf16)
            proj = _nm_matmul(x, norm1_g[l][None], mod[0], mod[1], w_in, "gla_in")
            wal = jnp.zeros((2, LANE, GLA_QK), f32)
            wal = wal.at[0, 0:GLA_RANK].set(gla_w_alpha[j, 0]).at[1, GLA_RANK:2 * GLA_RANK].set(gla_w_alpha[j, 1])
            bal = gla_b_alpha[j][:, None, :]
            ng = gla_norm_g[j][None]
            o_p, st = _gla_attn(proj, wal, bal, ng, None, N_PROMPT_SEQ, PROMPT_LEN, 0)
            gla_new.append(st)
            proj_s = _to_col_major(proj[NP:].reshape(N_SAMPLE_SEQ, SAMPLE_LEN, GLA_IN_PAD), rows).reshape(NS, GLA_IN_PAD)
            o_s, _ = _gla_attn(proj_s, wal, bal, ng, state_gla[:, j], N_SAMPLE_SEQ, SAMPLE_LEN, 0)
            o_s = _from_col_major(o_s.reshape(N_SAMPLE_SEQ, SAMPLE_LEN, GLA_V), rows).reshape(NS, GLA_V)
            y = jnp.concatenate([o_p, o_s], axis=0)
            x = _res_matmul(y, gla_w_out[j].astype(bf16), x, mod[2], "gla_out")
        xmt, cnt1, e1, rank2, e2 = _peer_route(x, norm2_g[l][None], mod[3], mod[4], peer_wq[l].T.astype(bf16),
                                             peer_k1[l].astype(bf16), peer_k2[l].astype(bf16))
        x = _peer_dense(xmt, peer_u, peer_v, l, cnt1, e1, rank2, e2, x, mod[5])

    y_prompt = _final_norm(x, final_norm_g[None], 0, NP).reshape(N_PROMPT_SEQ, PROMPT_LEN, D)
    y_sample = _final_norm(x, final_norm_g[None], NP, NS).reshape(N_SAMPLE_SEQ, SAMPLE_LEN, D)
    return (y_prompt, y_sample, jnp.stack(rg_new, axis=1), jnp.stack(gla_new, axis=1))
```
